```python
import math
import jax
import jax.numpy as jnp
from jax import lax

D_MODEL = 4096
BATCH = 8
SEQ = 2048
DEPTH = 4

N_EVEN = (DEPTH + 1) // 2
N_ODD = DEPTH // 2
N_VMIX = max(N_ODD - 1, 0)
RMS_EPS = 1e-6

ATTN_WIDTH = D_MODEL // 2
ATTN_HEAD_DIM = 128
ATTN_HEADS = ATTN_WIDTH // ATTN_HEAD_DIM
DILATED_BRANCHES = ((128, 1), (512, 4), (2048, 16))
ATTN_BLOCK = 128
ROPE_THETA = 10000.0

SSM_WIDTH = D_MODEL - ATTN_WIDTH
SSM_GROUP = 16
SSM_GROUPS = SSM_WIDTH // SSM_GROUP
SSM_STATE = 64
SSM_DT_MIN = 1e-3
SSM_DT_MAX = 1e-1

IN_WIDTH = 3 * ATTN_WIDTH + SSM_WIDTH

RWKV_HEAD_DIM = 64
RWKV_HEADS = D_MODEL // RWKV_HEAD_DIM
RWKV_DECAY_LORA = max(32, int(round(1.8 * D_MODEL ** 0.5 / 32)) * 32)
RWKV_AAA_LORA = max(32, int(round(1.8 * D_MODEL ** 0.5 / 32)) * 32)
RWKV_MV_LORA = max(32, int(round(1.3 * D_MODEL ** 0.5 / 32)) * 32)
RWKV_GATE_LORA = max(32, int(round(0.6 * D_MODEL ** 0.8 / 32)) * 32)
RWKV_GN_EPS = 64e-5

FFN_DIM = 256 * ((8 * D_MODEL // 3 + 255) // 256)
FFN_CONV = 3

kernel_name = 'hybrid_dilated_attn_s5_rwkv7_convglu'


def rmsnorm(x, gain):
    xf = x.astype(jnp.float32)
    y = xf * lax.rsqrt(jnp.mean(xf * xf, axis=-1, keepdims=True) + RMS_EPS)
    return (y * gain.astype(jnp.float32)).astype(x.dtype)


def rotary(x, positions):
    half = x.shape[-1] // 2
    inv_freq = ROPE_THETA ** (-jnp.arange(half, dtype=jnp.float32) / half)
    ang = positions.astype(jnp.float32)[:, None] * inv_freq[None, :]
    cos = jnp.cos(ang)[None, :, None, :]
    sin = jnp.sin(ang)[None, :, None, :]
    xf = x.astype(jnp.float32)
    x1, x2 = xf[..., :half], xf[..., half:]
    return jnp.concatenate([x1 * cos - x2 * sin, x1 * sin + x2 * cos], axis=-1).astype(x.dtype)


def dilated_branch(q, k, v, window, dilation):
    B, H, S, Dh = q.shape
    L = S // dilation
    span = window // dilation
    Q = ATTN_BLOCK
    nb = -(-L // Q)
    Lp = nb * Q

    def strided(t):
        t = t.reshape(B, H, L, dilation, Dh).transpose(0, 1, 3, 2, 4)
        return jnp.pad(t, ((0, 0), (0, 0), (0, 0), (0, Lp - L), (0, 0)))

    def band(t):
        tb = jnp.pad(t, ((0, 0), (0, 0), (0, 0), (Q, 0), (0, 0))).reshape(B, H, dilation, nb + 1, Q, Dh)
        return jnp.concatenate([tb[:, :, :, :-1], tb[:, :, :, 1:]], axis=4)

    qb = strided(q).reshape(B, H, dilation, nb, Q, Dh)
    kb = band(strided(k))
    vb = band(strided(v))
    qi = jnp.arange(Q)[:, None]
    ci = jnp.arange(2 * Q)[None, :]
    dist = qi + Q - ci
    key_pos = jnp.arange(nb)[:, None, None] * Q + ci[None] - Q
    valid = (dist >= 0) & (dist <= span) & (key_pos >= 0)
    s = jnp.einsum('bhrnqd,bhrnkd->bhrnqk', qb, kb) * (Dh ** -0.5)
    s = jnp.where(valid, s, -jnp.inf)
    lse = jax.nn.logsumexp(s, axis=-1, keepdims=True)
    o = jnp.einsum('bhrnqk,bhrnkd->bhrnqd', jnp.exp(s - lse), vb)
    o = o.reshape(B, H, dilation, Lp, Dh)[:, :, :, :L].transpose(0, 1, 3, 2, 4).reshape(B, H, S, Dh)
    lse = lse.reshape(B, H, dilation, Lp)[:, :, :, :L].transpose(0, 1, 3, 2).reshape(B, H, S)
    return o, lse


def dilated_attention(q, k, v):
    qf, kf, vf = (t.astype(jnp.float32).transpose(0, 2, 1, 3) for t in (q, k, v))
    outs, lses = [], []
    for window, dilation in DILATED_BRANCHES:
        o, lse = dilated_branch(qf, kf, vf, window, dilation)
        outs.append(o)
        lses.append(lse)
    weights = jax.nn.softmax(jnp.stack(lses), axis=0)
    o = jnp.sum(weights[..., None] * jnp.stack(outs), axis=0)
    return o.transpose(0, 2, 1, 3).astype(v.dtype)


def s5_glu(u, lam_re, lam_im, log_dt, b_re, b_im, c_re, c_im, d_skip, glu_w, glu_b):
    B, S, _ = u.shape
    uf = u.astype(jnp.float32)
    ug = uf.reshape(B, S, SSM_GROUPS, SSM_GROUP)
    lr, li = lam_re.astype(jnp.float32), lam_im.astype(jnp.float32)
    dt = jnp.exp(log_dt.astype(jnp.float32))[:, None]
    mag = jnp.exp(lr * dt)
    ab_re, ab_im = mag * jnp.cos(li * dt), mag * jnp.sin(li * dt)
    den = lr * lr + li * li
    nr, ni = ab_re - 1.0, ab_im
    coef_re = (nr * lr + ni * li) / den
    coef_im = (ni * lr - nr * li) / den
    br, bi = b_re.astype(jnp.float32), b_im.astype(jnp.float32)
    bb_re = coef_re[..., None] * br - coef_im[..., None] * bi
    bb_im = coef_re[..., None] * bi + coef_im[..., None] * br
    bu_re = jnp.einsum('bsgp,gnp->sbgn', ug, bb_re)
    bu_im = jnp.einsum('bsgp,gnp->sbgn', ug, bb_im)
    a_re = jnp.broadcast_to(ab_re[None, None], (S, 1) + ab_re.shape)
    a_im = jnp.broadcast_to(ab_im[None, None], (S, 1) + ab_im.shape)

    def combine(e1, e2):
        a1r, a1i, b1r, b1i = e1
        a2r, a2i, b2r, b2i = e2
        return (a2r * a1r - a2i * a1i, a2r * a1i + a2i * a1r,
                a2r * b1r - a2i * b1i + b2r, a2r * b1i + a2i * b1r + b2i)

    _, _, x_re, x_im = lax.associative_scan(combine, (a_re, a_im, bu_re, bu_im), axis=0)
    y = (jnp.einsum('sbgn,gpn->bsgp', x_re, c_re.astype(jnp.float32))
         - jnp.einsum('sbgn,gpn->bsgp', x_im, c_im.astype(jnp.float32)))
    y = y.reshape(B, S, SSM_WIDTH) + d_skip.astype(jnp.float32) * uf
    z = jax.nn.gelu(y, approximate=False)
    out = z * jax.nn.sigmoid(z @ glu_w.astype(jnp.float32) + glu_b.astype(jnp.float32))
    return out.astype(u.dtype)


def hybrid_attn_ssm(h, positions, w_in, q_norm, k_norm, lam_re, lam_im, log_dt,
                    b_re, b_im, c_re, c_im, d_skip, glu_w, glu_b, w_out):
    B, S, _ = h.shape
    q, k, v, u = jnp.split(h @ w_in, [ATTN_WIDTH, 2 * ATTN_WIDTH, 3 * ATTN_WIDTH], axis=-1)
    heads = lambda t: t.reshape(B, S, ATTN_HEADS, ATTN_HEAD_DIM)
    q = rotary(rmsnorm(heads(q), q_norm), positions)
    k = rotary(rmsnorm(heads(k), k_norm), positions)
    y_attn = dilated_attention(q, k, heads(v)).reshape(B, S, ATTN_WIDTH)
    y_ssm = s5_glu(u, lam_re, lam_im, log_dt, b_re, b_im, c_re, c_im, d_skip, glu_w, glu_b)
    return jnp.concatenate([y_attn, y_ssm], axis=-1) @ w_out


def wkv7_scan(r, w, k, v, a, b):
    B, S, H, N = r.shape
    xs = tuple(jnp.moveaxis(t, 1, 0) for t in (r, w, k, v, a, b))

    def step(state, inp):
        r_t, w_t, k_t, v_t, a_t, b_t = inp
        sa = jnp.einsum('bhij,bhj->bhi', state, a_t)
        state = (state * w_t[:, :, None, :] + sa[..., None] * b_t[:, :, None, :]
                 + v_t[..., None] * k_t[:, :, None, :])
        return state, jnp.einsum('bhij,bhj->bhi', state, r_t)

    _, ys = lax.scan(step, jnp.zeros((B, H, N, N), jnp.float32), xs)
    return jnp.moveaxis(ys, 0, 1)


def rwkv7_time_mix(h, mu, w_r, w_k, w_v, w0, w1, w2, a0, a1, a2, g1, g2,
                   k_k, k_a, r_k, ln_w, ln_b, w_o, v_first, v_mix):
    B, S, D = h.shape
    H, N = RWKV_HEADS, RWKV_HEAD_DIM
    xx = jnp.pad(h, ((0, 0), (1, 0), (0, 0)))[:, :-1] - h
    lerp = lambda i: h + xx * mu[i]
    r = lerp(0) @ w_r
    w = -jax.nn.softplus(-(w0 + jnp.tanh(lerp(1) @ w1) @ w2)) - 0.5
    k = lerp(2) @ w_k
    xv = lerp(3)
    v = xv @ w_v
    if v_mix is None:
        v_first = v
    else:
        v0, v1, v2 = v_mix
        v = v + (v_first - v) * jax.nn.sigmoid(v0 + (xv @ v1) @ v2)
    a = jax.nn.sigmoid(a0 + (lerp(4) @ a1) @ a2)
    g = jax.nn.sigmoid(lerp(5) @ g1) @ g2
    heads = lambda t: t.reshape(B, S, H, N).astype(jnp.float32)
    kk = heads(k * k_k)
    kk = kk / jnp.maximum(jnp.sqrt(jnp.sum(kk * kk, axis=-1, keepdims=True)), 1e-12)
    k = k * (1.0 + (a - 1.0) * k_a)
    rh, kh, vh, ah = heads(r), heads(k), heads(v), heads(a)
    decay = jnp.exp(-jnp.exp(heads(w)))
    y = wkv7_scan(rh, decay, kh, vh, -kk, kk * ah)
    mean = jnp.mean(y, axis=-1, keepdims=True)
    var = jnp.mean(jnp.square(y - mean), axis=-1, keepdims=True)
    y = ((y - mean) * lax.rsqrt(var + RWKV_GN_EPS)).reshape(B, S, D)
    y = y * ln_w.astype(jnp.float32) + ln_b.astype(jnp.float32)
    bonus = jnp.sum(rh * kh * r_k.astype(jnp.float32), axis=-1, keepdims=True) * vh
    y = (y.reshape(B, S, H, N) + bonus).reshape(B, S, D).astype(h.dtype)
    return (y * g) @ w_o, v_first


def causal_depthwise_conv(x, w, b):
    K, C = w.shape
    y = lax.conv_general_dilated(x, w[:, None, :].astype(x.dtype), window_strides=(1,),
                                 padding=((K - 1, 0),), dimension_numbers=('NWC', 'WIO', 'NWC'),
                                 feature_group_count=C)
    return y + b


def conv_glu_ffn(h, w_up, conv_w, conv_b, w_down):
    gate, val = jnp.split(h @ w_up, 2, axis=-1)
    gate = causal_depthwise_conv(gate, conv_w, conv_b)
    return (jax.nn.gelu(gate, approximate=False) * val) @ w_down


def setup_inputs(seed: int = 0) -> dict:
    key = jax.random.key(seed)
    ks = iter(jax.random.split(key, 64))
    f32 = jnp.float32
    normal = lambda shape, scale=1.0: scale * jax.random.normal(next(ks), shape, f32)
    uniform = lambda shape, lo, hi: jax.random.uniform(next(ks), shape, f32, lo, hi)
    D, F = D_MODEL, FFN_DIM
    NE, NO, NV = N_EVEN, N_ODD, N_VMIX
    G, N, P, SW = SSM_GROUPS, SSM_STATE, SSM_GROUP, SSM_WIDTH
    return {
        'x': normal((BATCH, SEQ, D)),
        'mix_norm': 1.0 + normal((DEPTH, D), 0.02),
        'hy_w_in': normal((NE, D, IN_WIDTH), D ** -0.5),
        'attn_q_norm': 1.0 + normal((NE, ATTN_HEAD_DIM), 0.02),
        'attn_k_norm': 1.0 + normal((NE, ATTN_HEAD_DIM), 0.02),
        'ssm_lambda_re': -0.5 + normal((NE, G, N), 0.01),
        'ssm_lambda_im': math.pi * jnp.arange(N, dtype=f32) + normal((NE, G, N), 0.01),
        'ssm_log_dt': uniform((NE, G), math.log(SSM_DT_MIN), math.log(SSM_DT_MAX)),
        'ssm_b_re': normal((NE, G, N, P), (2 * P) ** -0.5),
        'ssm_b_im': normal((NE, G, N, P), (2 * P) ** -0.5),
        'ssm_c_re': normal((NE, G, P, N), (2 * N) ** -0.5),
        'ssm_c_im': normal((NE, G, P, N), (2 * N) ** -0.5),
        'ssm_d': normal((NE, SW)),
        'ssm_glu_w': normal((NE, SW, SW), SW ** -0.5),
        'ssm_glu_b': normal((NE, SW), 0.02),
        'hy_w_out': normal((NE, ATTN_WIDTH + SW, D), (ATTN_WIDTH + SW) ** -0.5),
        'rwkv_mu': uniform((NO, 6, D), 0.0, 1.0),
        'rwkv_w_r': normal((NO, D, D), D ** -0.5),
        'rwkv_w_k': normal((NO, D, D), D ** -0.5),
        'rwkv_w_v': normal((NO, D, D), D ** -0.5),
        'rwkv_w0': uniform((NO, D), -6.5, -1.5),
        'rwkv_w1': normal((NO, D, RWKV_DECAY_LORA), D ** -0.5),
        'rwkv_w2': normal((NO, RWKV_DECAY_LORA, D), 0.3 * RWKV_DECAY_LORA ** -0.5),
        'rwkv_a0': normal((NO, D), 0.1),
        'rwkv_a1': normal((NO, D, RWKV_AAA_LORA), D ** -0.5),
        'rwkv_a2': normal((NO, RWKV_AAA_LORA, D), RWKV_AAA_LORA ** -0.5),
        'rwkv_g1': normal((NO, D, RWKV_GATE_LORA), D ** -0.5),
        'rwkv_g2': normal((NO, RWKV_GATE_LORA, D), RWKV_GATE_LORA ** -0.5),
        'rwkv_k_k': 0.85 + normal((NO, D), 0.02),
        'rwkv_k_a': 1.0 + normal((NO, D), 0.02),
        'rwkv_r_k': normal((NO, RWKV_HEADS, RWKV_HEAD_DIM), 0.1),
        'rwkv_ln_w': 1.0 + normal((NO, D), 0.02),
        'rwkv_ln_b': normal((NO, D), 0.02),
        'rwkv_w_o': normal((NO, D, D), D ** -0.5),
        'rwkv_v0': 1.0 + normal((NV, D), 0.1),
        'rwkv_v1': normal((NV, D, RWKV_MV_LORA), D ** -0.5),
        'rwkv_v2': normal((NV, RWKV_MV_LORA, D), RWKV_MV_LORA ** -0.5),
        'ffn_norm': 1.0 + normal((DEPTH, D), 0.02),
        'ffn_w_up': normal((DEPTH, D, 2 * F), D ** -0.5),
        'ffn_conv_w': normal((DEPTH, FFN_CONV, F), FFN_CONV ** -0.5),
        'ffn_conv_b': normal((DEPTH, F), 0.02),
        'ffn_w_down': normal((DEPTH, F, D), F ** -0.5),
    }


def reference(x, mix_norm, hy_w_in, attn_q_norm, attn_k_norm, ssm_lambda_re, ssm_lambda_im,
              ssm_log_dt, ssm_b_re, ssm_b_im, ssm_c_re, ssm_c_im, ssm_d, ssm_glu_w, ssm_glu_b,
              hy_w_out, rwkv_mu, rwkv_w_r, rwkv_w_k, rwkv_w_v, rwkv_w0, rwkv_w1, rwkv_w2,
              rwkv_a0, rwkv_a1, rwkv_a2, rwkv_g1, rwkv_g2, rwkv_k_k, rwkv_k_a, rwkv_r_k,
              rwkv_ln_w, rwkv_ln_b, rwkv_w_o, rwkv_v0, rwkv_v1, rwkv_v2,
              ffn_norm, ffn_w_up, ffn_conv_w, ffn_conv_b, ffn_w_down):
    positions = jnp.arange(x.shape[1], dtype=jnp.int32)
    v_first = None
    for layer in range(DEPTH):
        i = layer // 2
        h = rmsnorm(x, mix_norm[layer])
        if layer % 2 == 0:
            mixed = hybrid_attn_ssm(h, positions, hy_w_in[i], attn_q_norm[i], attn_k_norm[i],
                                    ssm_lambda_re[i], ssm_lambda_im[i], ssm_log_dt[i],
                                    ssm_b_re[i], ssm_b_im[i], ssm_c_re[i], ssm_c_im[i],
                                    ssm_d[i], ssm_glu_w[i], ssm_glu_b[i], hy_w_out[i])
        else:
            v_mix = None if i == 0 else (rwkv_v0[i - 1], rwkv_v1[i - 1], rwkv_v2[i - 1])
            mixed, v_first = rwkv7_time_mix(h, rwkv_mu[i], rwkv_w_r[i], rwkv_w_k[i], rwkv_w_v[i],
                                            rwkv_w0[i], rwkv_w1[i], rwkv_w2[i],
                                            rwkv_a0[i], rwkv_a1[i], rwkv_a2[i],
                                            rwkv_g1[i], rwkv_g2[i], rwkv_k_k[i], rwkv_k_a[i],
                                            rwkv_r_k[i], rwkv_ln_w[i], rwkv_ln_b[i], rwkv_w_o[i],
                                            v_first, v_mix)
        x = x + mixed
        x = x + conv_glu_ffn(rmsnorm(x, ffn_norm[layer]), ffn_w_up[layer], ffn_conv_w[layer],
                             ffn_conv_b[layer], ffn_w_down[layer])
    return x
```

```python
import functools
import math

import numpy as np
import jax
import jax.numpy as jnp
from jax import lax
from jax.experimental import pallas as pl
from jax.experimental.pallas import tpu as pltpu

F32 = jnp.float32
BF16 = jnp.bfloat16
HIGHEST = lax.Precision.HIGHEST

RMS_EPS = 1e-6
ATTN_HEAD_DIM = 128
DILATED_BRANCHES = ((128, 1), (512, 4), (2048, 16))
ROPE_THETA = 10000.0
SSM_GROUP = 16
SSM_STATE = 64
SSM_CHUNK = 16
RWKV_HEAD_DIM = 64
RWKV_GN_EPS = 64e-5
WKV_CHUNK = 64
FFN_CONV = 3

V7X_VMEM_BYTES = 64 * 1024 * 1024
VMEM_BUDGET = 48 * 1024 * 1024
LANE = 128
SUBLANE = 8

NT_DIMS = (((1,), (1,)), ((), ()))
TN_DIMS = (((0,), (0,)), ((), ()))


def _cparams(n_grid, vmem_bytes):
    limit = int(min(max(vmem_bytes, 16 * 1024 * 1024), V7X_VMEM_BYTES - 6 * 1024 * 1024))
    return pltpu.CompilerParams(dimension_semantics=("arbitrary",) * n_grid,
                                vmem_limit_bytes=limit)


def _pick_tile(n, pref, align):
    t = min(pref, n)
    t -= t % align
    while t >= align:
        if n % t == 0:
            return t
        t -= align
    return n


def _sigmoid(x):
    return 1.0 / (1.0 + jnp.exp(-x))


def _gelu(x):
    return 0.5 * x * (1.0 + lax.erf(x * np.float32(math.sqrt(0.5))))


def _rms(x, gain):
    return x * lax.rsqrt(jnp.mean(x * x, axis=-1, keepdims=True) + RMS_EPS) * gain


def _rmsnorm_body(x_ref, g_ref, o_ref):
    o_ref[...] = _rms(x_ref[...], g_ref[...]).astype(o_ref.dtype)


def _rmsnorm(x, gain, out_dtype=BF16):
    T, D = x.shape
    tm = _pick_tile(T, 256, SUBLANE)
    return pl.pallas_call(
        _rmsnorm_body,
        grid=(T // tm,),
        in_specs=[pl.BlockSpec((tm, D), lambda i: (i, 0)),
                  pl.BlockSpec((1, D), lambda i: (0, 0))],
        out_specs=pl.BlockSpec((tm, D), lambda i: (i, 0)),
        out_shape=jax.ShapeDtypeStruct((T, D), out_dtype),
        compiler_params=_cparams(1, 6 * tm * D * 4),
        name="rmsnorm",
    )(x, gain.reshape(1, D))


def _lerp_body(x_ref, xh_ref, g_ref, mu_ref, *o_refs, tm, seq):
    i = pl.program_id(0)
    g = g_ref[...]
    h = _rms(x_ref[...], g)
    hh = _rms(xh_ref[...], g)
    prev = jnp.where((i * tm) % seq == 0, 0.0, hh[SUBLANE - 1:SUBLANE, :])
    hs = pltpu.roll(h, 1, axis=0)
    row = lax.broadcasted_iota(jnp.int32, h.shape, 0)
    xx = jnp.where(row == 0, prev, hs) - h
    for n, o_ref in enumerate(o_refs):
        o_ref[...] = (h + xx * mu_ref[n:n + 1, :]).astype(o_ref.dtype)


def _rwkv_lerp(x, gain, mu, seq):
    T, D = x.shape
    n = mu.shape[0]
    tm = _pick_tile(seq, 128, SUBLANE)
    hb = tm // SUBLANE
    return pl.pallas_call(
        functools.partial(_lerp_body, tm=tm, seq=seq),
        grid=(T // tm,),
        in_specs=[pl.BlockSpec((tm, D), lambda i: (i, 0)),
                  pl.BlockSpec((SUBLANE, D), lambda i: (jnp.maximum(i * hb - 1, 0), 0)),
                  pl.BlockSpec((1, D), lambda i: (0, 0)),
                  pl.BlockSpec((n, D), lambda i: (0, 0))],
        out_specs=[pl.BlockSpec((tm, D), lambda i: (i, 0))] * n,
        out_shape=[jax.ShapeDtypeStruct((T, D), BF16)] * n,
        compiler_params=_cparams(1, tm * D * (2 * 4 + 2 * 2 * n + 4 * 4)),
        name="rwkv_lerp",
    )(x, x, gain.reshape(1, D), mu)


def _mm_body(*refs, n_pairs, has_bias, act, has_mul, has_res):
    acc = None
    idx = 0
    for _ in range(n_pairs):
        d = jnp.dot(refs[idx][...], refs[idx + 1][...], preferred_element_type=F32)
        acc = d if acc is None else acc + d
        idx += 2
    if has_bias:
        acc = acc + refs[idx][...]
        idx += 1
    if act == "tanh":
        acc = jnp.tanh(acc)
    elif act == "sigmoid":
        acc = _sigmoid(acc)
    if has_mul:
        acc = acc * refs[idx][...].astype(F32)
        idx += 1
    if has_res:
        acc = acc + refs[idx][...]
        idx += 1
    o_ref = refs[idx]
    o_ref[...] = acc.astype(o_ref.dtype)


def _mm_tiles(M, N, k_total, out_bytes, extra_bytes):
    for tm_pref, tn_pref in ((1024, 512), (512, 512), (512, 256), (256, 256), (128, 256), (128, 128)):
        tm = _pick_tile(M, tm_pref, 16)
        tn = _pick_tile(N, tn_pref, LANE)
        need = (2 * tm * k_total * 2 + 2 * k_total * tn * 2
                + tm * tn * (2 * out_bytes + 2 * extra_bytes + 8))
        if need <= VMEM_BUDGET:
            return tm, tn, need
    return tm, tn, need


def _matmul(pairs, *, out_dtype, bias=None, act=None, mul=None, residual=None, name="matmul"):
    M = pairs[0][0].shape[0]
    N = pairs[0][1].shape[1]
    k_total = sum(a.shape[1] for a, _ in pairs)
    out_bytes = jnp.dtype(out_dtype).itemsize
    extra = (2 if mul is not None else 0) + (4 if residual is not None else 0)
    tm, tn, need = _mm_tiles(M, N, k_total, out_bytes, extra)
    args, in_specs = [], []
    for a, w in pairs:
        K = a.shape[1]
        assert a.dtype == BF16 and w.dtype == BF16 and w.shape == (K, N) and a.shape[0] == M
        args += [a, w]
        in_specs += [pl.BlockSpec((tm, K), lambda i, j: (i, 0)),
                     pl.BlockSpec((K, tn), lambda i, j: (0, j))]
    if bias is not None:
        args.append(bias.reshape(1, N).astype(F32))
        in_specs.append(pl.BlockSpec((1, tn), lambda i, j: (0, j)))
    if mul is not None:
        args.append(mul)
        in_specs.append(pl.BlockSpec((tm, tn), lambda i, j: (i, j)))
    if residual is not None:
        args.append(residual)
        in_specs.append(pl.BlockSpec((tm, tn), lambda i, j: (i, j)))
    return pl.pallas_call(
        functools.partial(_mm_body, n_pairs=len(pairs), has_bias=bias is not None, act=act,
                          has_mul=mul is not None, has_res=residual is not None),
        grid=(M // tm, N // tn),
        in_specs=in_specs,
        out_specs=pl.BlockSpec((tm, tn), lambda i, j: (i, j)),
        out_shape=jax.ShapeDtypeStruct((M, N), out_dtype),
        compiler_params=_cparams(2, need + 8 * 1024 * 1024),
        name=name,
    )(*args)


def _attn_bias(S, tq):
    n_off = S // tq
    delta = (np.arange(n_off)[:, None, None] * tq
             + np.arange(tq)[None, :, None] - np.arange(tq)[None, None, :])
    mult = np.zeros(delta.shape, np.float64)
    for window, dilation in DILATED_BRANCHES:
        mult += (delta >= 0) & (delta <= window) & (delta % dilation == 0)
    with np.errstate(divide="ignore"):
        bias = np.where(mult > 0, np.log(np.maximum(mult, 1.0)), -1e30)
    return jnp.asarray(bias, F32)


def _rope_tables(S):
    half = ATTN_HEAD_DIM // 2
    inv_freq = ROPE_THETA ** (-jnp.arange(half, dtype=F32) / half)
    ang = jnp.arange(S, dtype=jnp.int32).astype(F32)[:, None] * inv_freq[None, :]
    cos, sin = jnp.cos(ang), jnp.sin(ang)
    return jnp.concatenate([cos, cos], axis=-1), jnp.concatenate([-sin, sin], axis=-1)


def _attn_body(q_ref, k_ref, v_ref, qg_ref, kg_ref, cos_ref, sin_ref, bias_ref, o_ref,
               qs_ref, ks_ref, *, S, tq):
    half = ATTN_HEAD_DIM // 2

    def prep(x_ref, g_ref, scale):
        x = _rms(x_ref[...].astype(F32), g_ref[...])
        x = x * cos_ref[...] + pltpu.roll(x, half, axis=1) * sin_ref[...]
        return (x * scale).astype(BF16)

    qs_ref[...] = prep(q_ref, qg_ref, np.float32(ATTN_HEAD_DIM ** -0.5))
    ks_ref[...] = prep(k_ref, kg_ref, np.float32(1.0))

    def q_block(i, _):
        q = qs_ref[pl.ds(pl.multiple_of(i * tq, tq), tq), :]

        def k_block(off, carry):
            m, l, acc = carry
            rows = pl.ds(pl.multiple_of((i - off) * tq, tq), tq)
            s = lax.dot_general(q, ks_ref[rows, :], NT_DIMS, preferred_element_type=F32)
            s = s + bias_ref[off]
            m_new = jnp.maximum(m, jnp.max(s, axis=-1, keepdims=True))
            alpha = jnp.exp(m - m_new)
            p = jnp.exp(s - m_new)
            l = alpha * l + jnp.sum(p, axis=-1, keepdims=True)
            acc = alpha * acc + jnp.dot(p.astype(BF16), v_ref[rows, :], preferred_element_type=F32)
            return m_new, l, acc

        init = (jnp.full((tq, 1), -1e30, F32), jnp.zeros((tq, 1), F32),
                jnp.zeros((tq, ATTN_HEAD_DIM), F32))
        _, l, acc = lax.fori_loop(0, i + 1, k_block, init)
        o_ref[pl.ds(pl.multiple_of(i * tq, tq), tq), :] = (acc / l).astype(o_ref.dtype)
        return 0

    lax.fori_loop(0, S // tq, q_block, 0)


def _attention(qkvu, q_gain, k_gain, B, S, n_heads):
    T = B * S
    dh = ATTN_HEAD_DIM
    tq = _pick_tile(S, 256, LANE)
    cos, sin = _rope_tables(S)
    bias = _attn_bias(S, tq)
    head = lambda off: pl.BlockSpec((S, dh), lambda b, h: (b, off + h))
    const2 = lambda shape: pl.BlockSpec(shape, lambda b, h: (0, 0))
    return pl.pallas_call(
        functools.partial(_attn_body, S=S, tq=tq),
        grid=(B, n_heads),
        in_specs=[head(0), head(n_heads), head(2 * n_heads),
                  const2((1, dh)), const2((1, dh)), const2((S, dh)), const2((S, dh)),
                  pl.BlockSpec(bias.shape, lambda b, h: (0, 0, 0))],
        out_specs=pl.BlockSpec((S, dh), lambda b, h: (b, h)),
        out_shape=jax.ShapeDtypeStruct((T, n_heads * dh), BF16),
        scratch_shapes=[pltpu.VMEM((S, dh), BF16), pltpu.VMEM((S, dh), BF16)],
        compiler_params=_cparams(2, 2 * bias.size * 4 + 24 * S * dh * 4),
        name="dilated_attention",
    )(qkvu, qkvu, qkvu, q_gain.reshape(1, dh), k_gain.reshape(1, dh), cos, sin, bias)


def _s5_chunk_operators(lam_re, lam_im, log_dt, b_re, b_im, c_re, c_im):
    Lc = SSM_CHUNK
    G, N = lam_re.shape
    P = b_re.shape[-1]
    dt = jnp.exp(log_dt)[:, None]
    mag = jnp.exp(lam_re * dt)
    ab_re, ab_im = mag * jnp.cos(lam_im * dt), mag * jnp.sin(lam_im * dt)
    den = lam_re * lam_re + lam_im * lam_im
    nr, ni = ab_re - 1.0, ab_im
    coef_re = (nr * lam_re + ni * lam_im) / den
    coef_im = (ni * lam_re - nr * lam_im) / den
    bb_re = coef_re[..., None] * b_re - coef_im[..., None] * b_im
    bb_im = coef_re[..., None] * b_im + coef_im[..., None] * b_re
    steps = jnp.arange(Lc + 1, dtype=F32)[:, None, None]
    pmag = jnp.exp(lam_re * dt * steps)
    pw_re, pw_im = pmag * jnp.cos(lam_im * dt * steps), pmag * jnp.sin(lam_im * dt * steps)
    ca_re = c_re[None] * pw_re[:Lc, :, None, :] - c_im[None] * pw_im[:Lc, :, None, :]
    ca_im = c_re[None] * pw_im[:Lc, :, None, :] + c_im[None] * pw_re[:Lc, :, None, :]
    lag = (jnp.sum(ca_re[..., None] * bb_re[None, :, None], axis=3)
           - jnp.sum(ca_im[..., None] * bb_im[None, :, None], axis=3))
    ti = jnp.arange(Lc)[:, None]
    to = jnp.arange(Lc)[None, :]
    sel = jnp.clip(to - ti, 0, Lc - 1)
    toep = jnp.where((to >= ti)[:, :, None, None, None], lag[sel], 0.0)
    toep = toep.transpose(2, 0, 4, 1, 3).reshape(G, Lc * P, Lc * P)
    rev = pw_re[Lc - 1 - jnp.arange(Lc)], pw_im[Lc - 1 - jnp.arange(Lc)]
    bin_re = rev[0][..., None] * bb_re[None] - rev[1][..., None] * bb_im[None]
    bin_im = rev[0][..., None] * bb_im[None] + rev[1][..., None] * bb_re[None]
    bin_re = bin_re.transpose(1, 0, 3, 2).reshape(G, Lc * P, N)
    bin_im = bin_im.transpose(1, 0, 3, 2).reshape(G, Lc * P, N)
    nxt_re, nxt_im = pw_re[1:], pw_im[1:]
    co_re = c_re[None] * nxt_re[:, :, None, :] - c_im[None] * nxt_im[:, :, None, :]
    co_im = c_re[None] * nxt_im[:, :, None, :] + c_im[None] * nxt_re[:, :, None, :]
    cout_re = co_re.transpose(1, 3, 0, 2).reshape(G, N, Lc * P)
    cout_im = (-co_im).transpose(1, 3, 0, 2).reshape(G, N, Lc * P)
    return (toep.astype(BF16), bin_re.astype(BF16), bin_im.astype(BF16),
            cout_re.astype(BF16), cout_im.astype(BF16),
            pw_re[Lc][:, None, :], pw_im[Lc][:, None, :])


def _s5_body(u_ref, toep_ref, bre_ref, bim_ref, cre_ref, cim_ref, are_ref, aim_ref, d_ref, z_ref,
             xin_re, xin_im, xst_re, xst_im, *, n_chunks, batch):
    u = u_ref[0]
    xin_re[...] = jnp.dot(u, bre_ref[0], preferred_element_type=F32)
    xin_im[...] = jnp.dot(u, bim_ref[0], preferred_element_type=F32)
    a_re, a_im = are_ref[0], aim_ref[0]

    def step(c, carry):
        x_re, x_im = carry
        rows = pl.ds(pl.multiple_of(c * batch, batch), batch)
        xst_re[rows, :] = x_re
        xst_im[rows, :] = x_im
        return (a_re * x_re - a_im * x_im + xin_re[rows, :],
                a_re * x_im + a_im * x_re + xin_im[rows, :])

    zero = jnp.zeros((batch, xin_re.shape[1]), F32)
    lax.fori_loop(0, n_chunks, step, (zero, zero))
    y = (jnp.dot(u, toep_ref[0], preferred_element_type=F32)
         + jnp.dot(xst_re[...].astype(BF16), cre_ref[0], preferred_element_type=F32)
         + jnp.dot(xst_im[...].astype(BF16), cim_ref[0], preferred_element_type=F32))
    y = y + d_ref[0] * u.astype(F32)
    z_ref[0] = _gelu(y).astype(z_ref.dtype)


def _s5(u, ops, d_skip, B, S):
    toep, bin_re, bin_im, cout_re, cout_im, a_re, a_im = ops
    G, W, N = bin_re.shape
    Lc, P = SSM_CHUNK, SSM_GROUP
    nc = S // Lc
    R = nc * B
    ut = u.reshape(B, nc, Lc, G, P).transpose(3, 1, 0, 2, 4).reshape(G, R, W)
    d_t = jnp.tile(d_skip.reshape(G, 1, P), (1, Lc, 1)).reshape(G, 1, W)
    per_g = lambda shape: pl.BlockSpec((1,) + shape, lambda g: (g, 0, 0))
    zt = pl.pallas_call(
        functools.partial(_s5_body, n_chunks=nc, batch=B),
        grid=(G,),
        in_specs=[per_g((R, W)), per_g((W, W)), per_g((W, N)), per_g((W, N)),
                  per_g((N, W)), per_g((N, W)), per_g((1, N)), per_g((1, N)), per_g((1, W))],
        out_specs=per_g((R, W)),
        out_shape=jax.ShapeDtypeStruct((G, R, W), BF16),
        scratch_shapes=[pltpu.VMEM((R, N), F32)] * 4,
        compiler_params=_cparams(1, R * W * 32 + R * N * 32),
        name="s5_chunked_scan",
    )(ut, toep, bin_re, bin_im, cout_re, cout_im, a_re, a_im, d_t)
    return zt.reshape(G, nc, B, Lc, P).transpose(2, 1, 3, 0, 4).reshape(B * S, G * P)


def _ffn_act_body(g_ref, v_ref, w_ref, b_ref, o_ref):
    g = g_ref[...].astype(F32)
    row = lax.broadcasted_iota(jnp.int32, g.shape, 0)
    y = w_ref[FFN_CONV - 1:FFN_CONV, :] * g + b_ref[...]
    for back in range(1, FFN_CONV):
        shifted = jnp.where(row >= back, pltpu.roll(g, back, axis=0), 0.0)
        y = y + w_ref[FFN_CONV - 1 - back:FFN_CONV - back, :] * shifted
    o_ref[...] = (_gelu(y) * v_ref[...].astype(F32)).astype(o_ref.dtype)


def _ffn_act(gv, conv_w, conv_b, B, S):
    T, F2 = gv.shape
    Fd = F2 // 2
    tc = _pick_tile(Fd, 256, LANE)
    nj = Fd // tc
    return pl.pallas_call(
        _ffn_act_body,
        grid=(B, nj),
        in_specs=[pl.BlockSpec((S, tc), lambda b, j: (b, j)),
                  pl.BlockSpec((S, tc), lambda b, j: (b, nj + j)),
                  pl.BlockSpec((FFN_CONV, tc), lambda b, j: (0, j)),
                  pl.BlockSpec((1, tc), lambda b, j: (0, j))],
        out_specs=pl.BlockSpec((S, tc), lambda b, j: (b, j)),
        out_shape=jax.ShapeDtypeStruct((T, Fd), BF16),
        compiler_params=_cparams(2, S * tc * 48),
        name="convglu_act",
    )(gv, gv, conv_w, conv_b.reshape(1, Fd))


def _wkv_body(*refs, n_chunks, has_vmix):
    C = WKV_CHUNK
    nd = RWKV_HEAD_DIM
    if has_vmix:
        (r_ref, w_ref, k_ref, v_ref, a_ref, g_ref, vf_ref, vg_ref,
         kk_ref, ka_ref, rk_ref, lnw_ref, lnb_ref, o_ref) = refs
    else:
        (r_ref, w_ref, k_ref, v_ref, a_ref, g_ref,
         kk_ref, ka_ref, rk_ref, lnw_ref, lnb_ref, o_ref) = refs
    lanes = 2 * nd
    lane_r = lax.broadcasted_iota(jnp.int32, (lanes, lanes), 0)
    lane_c = lax.broadcasted_iota(jnp.int32, (lanes, lanes), 1)
    head_ones = ((lane_r // nd) == (lane_c // nd)).astype(F32)
    strict = lane_r > lane_c
    incl = lane_r >= lane_c
    eye = (lane_r == lane_c).astype(F32)
    tri = (lax.broadcasted_iota(jnp.int32, (C, C), 0)
           >= lax.broadcasted_iota(jnp.int32, (C, C), 1)).astype(F32)
    lane1 = lax.broadcasted_iota(jnp.int32, (1, lanes), 1)
    m0 = (lane1 < nd).astype(F32)
    m1 = 1.0 - m0
    k_k, k_a, r_k = kk_ref[...], ka_ref[...], rk_ref[...]
    ln_w, ln_b = lnw_ref[...], lnb_ref[...]

    def head_sum(x):
        return jnp.dot(x, head_ones, precision=HIGHEST, preferred_element_type=F32)

    def stack(x):
        return jnp.concatenate([x * m0, x * m1], axis=0)

    def bdot(a, b, dims=None):
        a, b = a.astype(BF16), b.astype(BF16)
        if dims is None:
            return jnp.dot(a, b, preferred_element_type=F32)
        return lax.dot_general(a, b, dims, preferred_element_type=F32)

    def chunk(c, state):
        rows = pl.ds(pl.multiple_of(c * C, C), C)
        r, k, v, a = r_ref[rows, :], k_ref[rows, :], v_ref[rows, :], a_ref[rows, :]
        wpre = w_ref[rows, :]
        wlog = -(jnp.maximum(-wpre, 0.0) + jnp.log(1.0 + jnp.exp(-jnp.abs(wpre)))) - 0.5
        dlog = -jnp.exp(wlog)
        kk = k * k_k
        kk = kk / jnp.maximum(jnp.sqrt(head_sum(kk * kk)), 1e-12)
        k2 = k * (1.0 + (a - 1.0) * k_a)
        if has_vmix:
            v = v + (vf_ref[rows, :] - v) * vg_ref[rows, :]
        bonus = head_sum(r * k2 * r_k) * v
        av, bv = -kk, kk * a

        lg = jnp.dot(tri, dlog, precision=HIGHEST, preferred_element_type=F32)
        lg_end = lg[C - 1:C, :]
        e_neg = jnp.exp(-lg)
        e_end = jnp.exp(lg_end - lg)
        at2 = stack(av * jnp.exp(lg - dlog))
        rt2 = stack(r * jnp.exp(lg))
        bt2, kt2 = stack(bv * e_neg), stack(k2 * e_neg)
        bh2, kh2 = stack(bv * e_end), stack(k2 * e_end)
        v2 = stack(v)

        left = jnp.concatenate([at2, rt2], axis=0)
        aa = bdot(left, jnp.concatenate([bt2, kt2], axis=0), NT_DIMS)
        a_ab = jnp.where(strict, aa[:2 * C, :2 * C], 0.0)
        a_ak = jnp.where(strict, aa[:2 * C, 2 * C:], 0.0)
        a_rb = jnp.where(incl, aa[2 * C:, :2 * C], 0.0)
        a_rk = jnp.where(incl, aa[2 * C:, 2 * C:], 0.0)

        inv = eye + a_ab
        power = a_ab
        for _ in range(int(math.log2(C)) - 1):
            power = bdot(power, power)
            inv = inv + bdot(inv, power)

        ls = bdot(left, state, NT_DIMS)
        u2 = bdot(inv, ls[:2 * C] + bdot(a_ak, v2))
        uv = jnp.concatenate([u2, v2], axis=0)
        y2 = ls[2 * C:] + bdot(jnp.concatenate([a_rb, a_rk], axis=1), uv)
        y = y2[:C] + y2[C:]

        mean = head_sum(y) * np.float32(1.0 / nd)
        d = y - mean
        var = head_sum(d * d) * np.float32(1.0 / nd)
        yn = d * lax.rsqrt(var + RWKV_GN_EPS) * ln_w + ln_b
        o_ref[rows, :] = ((yn + bonus) * g_ref[rows, :].astype(F32)).astype(o_ref.dtype)

        return state * jnp.exp(lg_end) + bdot(uv, jnp.concatenate([bh2, kh2], axis=0), TN_DIMS)

    lax.fori_loop(0, n_chunks, chunk, jnp.zeros((lanes, lanes), F32))


def _wkv(r, wpre, k, v, a, g, v_first, v_gate, k_k, k_a, r_k, ln_w, ln_b, B, S):
    T, D = r.shape
    lanes = 2 * RWKV_HEAD_DIM
    has_vmix = v_first is not None
    seq = pl.BlockSpec((S, lanes), lambda b, h: (b, h))
    par = pl.BlockSpec((1, lanes), lambda b, h: (0, h))
    args = [r, wpre, k, v, a, g] + ([v_first, v_gate] if has_vmix else [])
    params = [p.reshape(1, D) for p in (k_k, k_a, r_k, ln_w, ln_b)]
    return pl.pallas_call(
        functools.partial(_wkv_body, n_chunks=S // WKV_CHUNK, has_vmix=has_vmix),
        grid=(B, D // lanes),
        in_specs=[seq] * len(args) + [par] * len(params),
        out_specs=seq,
        out_shape=jax.ShapeDtypeStruct((T, D), BF16),
        compiler_params=_cparams(2, 2 * S * lanes * 4 * (len(args) + 1) + 8 * 1024 * 1024),
        name="wkv7_chunked",
    )(*args, *params)


def _pad_to(x, n, axis):
    pad = n - x.shape[axis]
    if pad == 0:
        return x
    widths = [(0, 0)] * x.ndim
    widths[axis] = (0, pad)
    return jnp.pad(x, widths)


def _lora(x, w_down, w_up, *, bias=None, act_mid=None, act_out=None, out_dtype=F32, name="lora"):
    inner = -(-w_down.shape[1] // LANE) * LANE
    wd = _pad_to(w_down, inner, 1).astype(BF16)
    wu = _pad_to(w_up, inner, 0).astype(BF16)
    mid = _matmul([(x, wd)], out_dtype=BF16, act=act_mid, name=name + "_down")
    return _matmul([(mid, wu)], out_dtype=out_dtype, bias=bias, act=act_out, name=name + "_up")


def _hybrid_layer(x, B, S, norm, w_in, q_norm, k_norm, lam_re, lam_im, log_dt, b_re, b_im, c_re, c_im,
                  d_skip, glu_w, glu_b, w_out):
    aw = w_out.shape[0] - d_skip.shape[0]
    n_heads = aw // ATTN_HEAD_DIM
    h = _rmsnorm(x, norm)
    qkvu = _matmul([(h, w_in.astype(BF16))], out_dtype=BF16, name="hybrid_in_proj")
    y_attn = _attention(qkvu, q_norm, k_norm, B, S, n_heads)
    ops = _s5_chunk_operators(lam_re, lam_im, log_dt, b_re, b_im, c_re, c_im)
    z = _s5(qkvu[:, 3 * aw:], ops, d_skip, B, S)
    y_ssm = _matmul([(z, glu_w.astype(BF16))], out_dtype=BF16, bias=glu_b, act="sigmoid", mul=z,
                    name="ssm_glu")
    w_out = w_out.astype(BF16)
    return _matmul([(y_attn, w_out[:aw]), (y_ssm, w_out[aw:])], out_dtype=F32, residual=x,
                   name="hybrid_out_proj")


def _rwkv_layer(x, B, S, norm, mu, w_r, w_k, w_v, w0, w1, w2, a0, a1, a2, g1, g2, k_k, k_a, r_k,
                ln_w, ln_b, w_o, v_first, v_mix):
    l_r, l_w, l_k, l_v, l_a, l_g = _rwkv_lerp(x, norm, mu, S)
    r = _matmul([(l_r, w_r.astype(BF16))], out_dtype=F32, name="rwkv_r")
    k = _matmul([(l_k, w_k.astype(BF16))], out_dtype=F32, name="rwkv_k")
    v = _matmul([(l_v, w_v.astype(BF16))], out_dtype=F32, name="rwkv_v")
    wpre = _lora(l_w, w1, w2, bias=w0, act_mid="tanh", name="rwkv_decay")
    a = _lora(l_a, a1, a2, bias=a0, act_out="sigmoid", name="rwkv_iclr")
    g = _lora(l_g, g1, g2, act_mid="sigmoid", out_dtype=BF16, name="rwkv_gate")
    if v_mix is None:
        v_gate, vf = None, None
        v_first = v
    else:
        v0, v1, v2 = v_mix
        v_gate = _lora(l_v, v1, v2, bias=v0, act_out="sigmoid", name="rwkv_vmix")
        vf = v_first
    y = _wkv(r, wpre, k, v, a, g, vf, v_gate, k_k, k_a, r_k.reshape(-1), ln_w, ln_b, B, S)
    return _matmul([(y, w_o.astype(BF16))], out_dtype=F32, residual=x, name="rwkv_out_proj"), v_first


def _ffn_layer(x, B, S, norm, w_up, conv_w, conv_b, w_down):
    h = _rmsnorm(x, norm)
    gv = _matmul([(h, w_up.astype(BF16))], out_dtype=BF16, name="ffn_up")
    act = _ffn_act(gv, conv_w, conv_b, B, S)
    return _matmul([(act, w_down.astype(BF16))], out_dtype=F32, residual=x, name="ffn_down")


def kernel(x, mix_norm, hy_w_in, attn_q_norm, attn_k_norm, ssm_lambda_re, ssm_lambda_im, ssm_log_dt, ssm_b_re, ssm_b_im, ssm_c_re, ssm_c_im, ssm_d, ssm_glu_w, ssm_glu_b, hy_w_out, rwkv_mu, rwkv_w_r, rwkv_w_k, rwkv_w_v, rwkv_w0, rwkv_w1, rwkv_w2, rwkv_a0, rwkv_a1, rwkv_a2, rwkv_g1, rwkv_g2, rwkv_k_k, rwkv_k_a, rwkv_r_k, rwkv_ln_w, rwkv_ln_b, rwkv_w_o, rwkv_v0, rwkv_v1, rwkv_v2, ffn_norm, ffn_w_up, ffn_conv_w, ffn_conv_b, ffn_w_down):
    B, S, D = x.shape
    depth = mix_norm.shape[0]
    x = x.reshape(B * S, D)
    v_first = None
    for layer in range(depth):
        i = layer // 2
        if layer % 2 == 0:
            x = _hybrid_layer(x, B, S, mix_norm[layer], hy_w_in[i], attn_q_norm[i], attn_k_norm[i],
                              ssm_lambda_re[i], ssm_lambda_im[i], ssm_log_dt[i], ssm_b_re[i],
                              ssm_b_im[i], ssm_c_re[i], ssm_c_im[i], ssm_d[i], ssm_glu_w[i],
                              ssm_glu_b[i], hy_w_out[i])
        else:
            v_mix = None if i == 0 else (rwkv_v0[i - 1], rwkv_v1[i - 1], rwkv_v2[i - 1])
            x, v_first = _rwkv_layer(x, B, S, mix_norm[layer], rwkv_mu[i], rwkv_w_r[i], rwkv_w_k[i],
                                     rwkv_w_v[i], rwkv_w0[i], rwkv_w1[i], rwkv_w2[i], rwkv_a0[i],
                                     rwkv_a1[i], rwkv_a2[i], rwkv_g1[i], rwkv_g2[i], rwkv_k_k[i],
                                     rwkv_k_a[i], rwkv_r_k[i], rwkv_ln_w[i], rwkv_ln_b[i],
                                     rwkv_w_o[i], v_first, v_mix)
        x = _ffn_layer(x, B, S, ffn_norm[layer], ffn_w_up[layer], ffn_conv_w[layer],
                       ffn_conv_b[layer], ffn_w_down[layer])
    return x.reshape(B, S, D)
```

```python
import functools
import math

import numpy as np
import jax
import jax.numpy as jnp
from jax import lax
from jax.experimental import pallas as pl
from jax.experimental.pallas import tpu as pltpu

F32 = jnp.float32
BF16 = jnp.bfloat16
HIGHEST = lax.Precision.HIGHEST

RMS_EPS = 1e-6
ATTN_HEAD_DIM = 128
DILATED_BRANCHES = ((128, 1), (512, 4), (2048, 16))
ROPE_THETA = 10000.0
SSM_GROUP = 16
SSM_STATE = 64
SSM_CHUNK = 16
RWKV_HEAD_DIM = 64
RWKV_GN_EPS = 64e-5
WKV_CHUNK = 64
FFN_CONV = 3

V7X_VMEM_BYTES = 64 * 1024 * 1024
VMEM_BUDGET = 48 * 1024 * 1024
LANE = 128
SUBLANE = 8

NT_DIMS = (((1,), (1,)), ((), ()))
TN_DIMS = (((0,), (0,)), ((), ()))


def _cparams(n_grid, vmem_bytes):
    limit = int(min(max(vmem_bytes, 16 * 1024 * 1024), V7X_VMEM_BYTES - 6 * 1024 * 1024))
    return pltpu.CompilerParams(dimension_semantics=("arbitrary",) * n_grid,
                                vmem_limit_bytes=limit)


def _pick_tile(n, pref, align):
    t = min(pref, n)
    t -= t % align
    while t >= align:
        if n % t == 0:
            return t
        t -= align
    return n


def _sigmoid(x):
    return 1.0 / (1.0 + jnp.exp(-x))


def _gelu(x):
    return 0.5 * x * (1.0 + lax.erf(x * np.float32(math.sqrt(0.5))))


def _rms(x, gain):
    return x * lax.rsqrt(jnp.mean(x * x, axis=-1, keepdims=True) + RMS_EPS) * gain


def _rmsnorm_body(x_ref, g_ref, o_ref):
    o_ref[...] = _rms(x_ref[...], g_ref[...]).astype(o_ref.dtype)


def _rmsnorm(x, gain, out_dtype=BF16):
    T, D = x.shape
    tm = _pick_tile(T, 256, SUBLANE)
    return pl.pallas_call(
        _rmsnorm_body,
        grid=(T // tm,),
        in_specs=[pl.BlockSpec((tm, D), lambda i: (i, 0)),
                  pl.BlockSpec((1, D), lambda i: (0, 0))],
        out_specs=pl.BlockSpec((tm, D), lambda i: (i, 0)),
        out_shape=jax.ShapeDtypeStruct((T, D), out_dtype),
        compiler_params=_cparams(1, 6 * tm * D * 4),
        name="rmsnorm",
    )(x, gain.reshape(1, D))


def _lerp_body(x_ref, xh_ref, g_ref, mu_ref, *o_refs, tm, seq):
    i = pl.program_id(0)
    g = g_ref[...]
    h = _rms(x_ref[...], g)
    hh = _rms(xh_ref[...], g)
    prev = jnp.where((i * tm) % seq == 0, 0.0, hh[SUBLANE - 1:SUBLANE, :])
    hs = pltpu.roll(h, 1, axis=0)
    row = lax.broadcasted_iota(jnp.int32, h.shape, 0)
    xx = jnp.where(row == 0, prev, hs) - h
    for n, o_ref in enumerate(o_refs):
        o_ref[...] = (h + xx * mu_ref[n:n + 1, :]).astype(o_ref.dtype)


def _rwkv_lerp(x, gain, mu, seq):
    T, D = x.shape
    n = mu.shape[0]
    tm = _pick_tile(seq, 128, SUBLANE)
    hb = tm // SUBLANE
    return pl.pallas_call(
        functools.partial(_lerp_body, tm=tm, seq=seq),
        grid=(T // tm,),
        in_specs=[pl.BlockSpec((tm, D), lambda i: (i, 0)),
                  pl.BlockSpec((SUBLANE, D), lambda i: (jnp.maximum(i * hb - 1, 0), 0)),
                  pl.BlockSpec((1, D), lambda i: (0, 0)),
                  pl.BlockSpec((n, D), lambda i: (0, 0))],
        out_specs=[pl.BlockSpec((tm, D), lambda i: (i, 0))] * n,
        out_shape=[jax.ShapeDtypeStruct((T, D), BF16)] * n,
        compiler_params=_cparams(1, tm * D * (2 * 4 + 2 * 2 * n + 4 * 4)),
        name="rwkv_lerp",
    )(x, x, gain.reshape(1, D), mu)


def _mm_body(*refs, n_pairs, has_bias, act, has_mul, has_res):
    acc = None
    idx = 0
    for _ in range(n_pairs):
        d = jnp.dot(refs[idx][...], refs[idx + 1][...], preferred_element_type=F32)
        acc = d if acc is None else acc + d
        idx += 2
    if has_bias:
        acc = acc + refs[idx][...]
        idx += 1
    if act == "tanh":
        acc = jnp.tanh(acc)
    elif act == "sigmoid":
        acc = _sigmoid(acc)
    if has_mul:
        acc = acc * refs[idx][...].astype(F32)
        idx += 1
    if has_res:
        acc = acc + refs[idx][...]
        idx += 1
    o_ref = refs[idx]
    o_ref[...] = acc.astype(o_ref.dtype)


def _mm_tiles(M, N, k_total, out_bytes, extra_bytes):
    for tm_pref, tn_pref in ((1024, 512), (512, 512), (512, 256), (256, 256), (128, 256), (128, 128)):
        tm = _pick_tile(M, tm_pref, 16)
        tn = _pick_tile(N, tn_pref, LANE)
        need = (2 * tm * k_total * 2 + 2 * k_total * tn * 2
                + tm * tn * (2 * out_bytes + 2 * extra_bytes + 8))
        if need <= VMEM_BUDGET:
            return tm, tn, need
    return tm, tn, need


def _matmul(pairs, *, out_dtype, bias=None, act=None, mul=None, residual=None, name="matmul"):
    M = pairs[0][0].shape[0]
    N = pairs[0][1].shape[1]
    k_total = sum(a.shape[1] for a, _ in pairs)
    out_bytes = jnp.dtype(out_dtype).itemsize
    extra = (2 if mul is not None else 0) + (4 if residual is not None else 0)
    tm, tn, need = _mm_tiles(M, N, k_total, out_bytes, extra)
    args, in_specs = [], []
    for a, w in pairs:
        K = a.shape[1]
        assert a.dtype == BF16 and w.dtype == BF16 and w.shape == (K, N) and a.shape[0] == M
        args += [a, w]
        in_specs += [pl.BlockSpec((tm, K), lambda i, j: (i, 0)),
                     pl.BlockSpec((K, tn), lambda i, j: (0, j))]
    if bias is not None:
        args.append(bias.reshape(1, N).astype(F32))
        in_specs.append(pl.BlockSpec((1, tn), lambda i, j: (0, j)))
    if mul is not None:
        args.append(mul)
        in_specs.append(pl.BlockSpec((tm, tn), lambda i, j: (i, j)))
    if residual is not None:
        args.append(residual)
        in_specs.append(pl.BlockSpec((tm, tn), lambda i, j: (i, j)))
    return pl.pallas_call(
        functools.partial(_mm_body, n_pairs=len(pairs), has_bias=bias is not None, act=act,
                          has_mul=mul is not None, has_res=residual is not None),
        grid=(M // tm, N // tn),
        in_specs=in_specs,
        out_specs=pl.BlockSpec((tm, tn), lambda i, j: (i, j)),
        out_shape=jax.ShapeDtypeStruct((M, N), out_dtype),
        compiler_params=_cparams(2, need + 8 * 1024 * 1024),
        name=name,
    )(*args)


def _attn_bias(S, tq):
    delta = (S - tq) + np.arange(tq)[:, None] - np.arange(S)[None, :]
    mult = np.zeros(delta.shape, np.float64)
    for window, dilation in DILATED_BRANCHES:
        mult += (delta >= 0) & (delta <= window) & (delta % dilation == 0)
    bias = np.where(mult > 0, np.log(np.maximum(mult, 1.0)), -1e30)
    return jnp.asarray(bias, F32)


def _rope_tables(S):
    half = ATTN_HEAD_DIM // 2
    inv_freq = ROPE_THETA ** (-jnp.arange(half, dtype=F32) / half)
    ang = jnp.arange(S, dtype=jnp.int32).astype(F32)[:, None] * inv_freq[None, :]
    cos, sin = jnp.cos(ang), jnp.sin(ang)
    return jnp.concatenate([cos, cos], axis=-1), jnp.concatenate([-sin, sin], axis=-1)


def _attn_body(q_ref, k_ref, v_ref, qg_ref, kg_ref, cos_ref, sin_ref, bias_ref, o_ref,
               qs_ref, ks_ref, *, S, tq):
    half = ATTN_HEAD_DIM // 2

    def prep(x_ref, g_ref, scale):
        x = _rms(x_ref[...].astype(F32), g_ref[...])
        x = x * cos_ref[...] + pltpu.roll(x, half, axis=1) * sin_ref[...]
        return (x * scale).astype(BF16)

    qs_ref[...] = prep(q_ref, qg_ref, np.float32(ATTN_HEAD_DIM ** -0.5))
    ks_ref[...] = prep(k_ref, kg_ref, np.float32(1.0))

    n_blocks = S // tq

    def scores(i):
        return lax.dot_general(qs_ref[i * tq:(i + 1) * tq, :], ks_ref[:(i + 1) * tq, :], NT_DIMS,
                               preferred_element_type=F32)

    s_next = scores(0)
    for i in range(n_blocks):
        s = s_next + bias_ref[:, (n_blocks - 1 - i) * tq:]
        if i + 1 < n_blocks:
            s_next = scores(i + 1)
        p = jnp.exp(s - jnp.max(s, axis=-1, keepdims=True))
        l = jnp.sum(p, axis=-1, keepdims=True)
        acc = jnp.dot(p.astype(BF16), v_ref[:(i + 1) * tq, :], preferred_element_type=F32)
        o_ref[i * tq:(i + 1) * tq, :] = (acc / l).astype(o_ref.dtype)


def _attention(qkvu, q_gain, k_gain, B, S, n_heads):
    T = B * S
    dh = ATTN_HEAD_DIM
    tq = _pick_tile(S, 256, LANE)
    cos, sin = _rope_tables(S)
    bias = _attn_bias(S, tq)
    head = lambda off: pl.BlockSpec((S, dh), lambda b, h: (b, off + h))
    const2 = lambda shape: pl.BlockSpec(shape, lambda b, h: (0, 0))
    return pl.pallas_call(
        functools.partial(_attn_body, S=S, tq=tq),
        grid=(B, n_heads),
        in_specs=[head(0), head(n_heads), head(2 * n_heads),
                  const2((1, dh)), const2((1, dh)), const2((S, dh)), const2((S, dh)),
                  const2(bias.shape)],
        out_specs=pl.BlockSpec((S, dh), lambda b, h: (b, h)),
        out_shape=jax.ShapeDtypeStruct((T, n_heads * dh), BF16),
        scratch_shapes=[pltpu.VMEM((S, dh), BF16), pltpu.VMEM((S, dh), BF16)],
        compiler_params=_cparams(2, 2 * bias.size * 4 + 24 * S * dh * 4),
        name="dilated_attention",
    )(qkvu, qkvu, qkvu, q_gain.reshape(1, dh), k_gain.reshape(1, dh), cos, sin, bias)


def _s5_chunk_operators(lam_re, lam_im, log_dt, b_re, b_im, c_re, c_im):
    Lc = SSM_CHUNK
    G, N = lam_re.shape
    P = b_re.shape[-1]
    dt = jnp.exp(log_dt)[:, None]
    mag = jnp.exp(lam_re * dt)
    ab_re, ab_im = mag * jnp.cos(lam_im * dt), mag * jnp.sin(lam_im * dt)
    den = lam_re * lam_re + lam_im * lam_im
    nr, ni = ab_re - 1.0, ab_im
    coef_re = (nr * lam_re + ni * lam_im) / den
    coef_im = (ni * lam_re - nr * lam_im) / den
    bb_re = coef_re[..., None] * b_re - coef_im[..., None] * b_im
    bb_im = coef_re[..., None] * b_im + coef_im[..., None] * b_re
    steps = jnp.arange(Lc + 1, dtype=F32)[:, None, None]
    pmag = jnp.exp(lam_re * dt * steps)
    pw_re, pw_im = pmag * jnp.cos(lam_im * dt * steps), pmag * jnp.sin(lam_im * dt * steps)
    ca_re = c_re[None] * pw_re[:Lc, :, None, :] - c_im[None] * pw_im[:Lc, :, None, :]
    ca_im = c_re[None] * pw_im[:Lc, :, None, :] + c_im[None] * pw_re[:Lc, :, None, :]
    lag = (jnp.sum(ca_re[..., None] * bb_re[None, :, None], axis=3)
           - jnp.sum(ca_im[..., None] * bb_im[None, :, None], axis=3))
    ti = jnp.arange(Lc)[:, None]
    to = jnp.arange(Lc)[None, :]
    sel = jnp.clip(to - ti, 0, Lc - 1)
    toep = jnp.where((to >= ti)[:, :, None, None, None], lag[sel], 0.0)
    toep = toep.transpose(2, 0, 4, 1, 3).reshape(G, Lc * P, Lc * P)
    rev = pw_re[Lc - 1 - jnp.arange(Lc)], pw_im[Lc - 1 - jnp.arange(Lc)]
    bin_re = rev[0][..., None] * bb_re[None] - rev[1][..., None] * bb_im[None]
    bin_im = rev[0][..., None] * bb_im[None] + rev[1][..., None] * bb_re[None]
    bin_re = bin_re.transpose(1, 0, 3, 2).reshape(G, Lc * P, N)
    bin_im = bin_im.transpose(1, 0, 3, 2).reshape(G, Lc * P, N)
    nxt_re, nxt_im = pw_re[1:], pw_im[1:]
    co_re = c_re[None] * nxt_re[:, :, None, :] - c_im[None] * nxt_im[:, :, None, :]
    co_im = c_re[None] * nxt_im[:, :, None, :] + c_im[None] * nxt_re[:, :, None, :]
    cout_re = co_re.transpose(1, 3, 0, 2).reshape(G, N, Lc * P)
    cout_im = (-co_im).transpose(1, 3, 0, 2).reshape(G, N, Lc * P)
    return (toep.astype(BF16), bin_re.astype(BF16), bin_im.astype(BF16),
            cout_re.astype(BF16), cout_im.astype(BF16),
            pw_re[Lc][:, None, :], pw_im[Lc][:, None, :])


def _s5_body(u_ref, toep_ref, bre_ref, bim_ref, cre_ref, cim_ref, are_ref, aim_ref, d_ref, z_ref,
             xin_re, xin_im, xst_re, xst_im, *, n_chunks, batch):
    u = u_ref[0]
    xin_re[...] = jnp.dot(u, bre_ref[0], preferred_element_type=F32)
    xin_im[...] = jnp.dot(u, bim_ref[0], preferred_element_type=F32)
    a_re, a_im = are_ref[0], aim_ref[0]

    def step(c, carry):
        x_re, x_im = carry
        rows = pl.ds(pl.multiple_of(c * batch, batch), batch)
        xst_re[rows, :] = x_re
        xst_im[rows, :] = x_im
        return (a_re * x_re - a_im * x_im + xin_re[rows, :],
                a_re * x_im + a_im * x_re + xin_im[rows, :])

    zero = jnp.zeros((batch, xin_re.shape[1]), F32)
    lax.fori_loop(0, n_chunks, step, (zero, zero))
    y = (jnp.dot(u, toep_ref[0], preferred_element_type=F32)
         + jnp.dot(xst_re[...].astype(BF16), cre_ref[0], preferred_element_type=F32)
         + jnp.dot(xst_im[...].astype(BF16), cim_ref[0], preferred_element_type=F32))
    y = y + d_ref[0] * u.astype(F32)
    z_ref[0] = _gelu(y).astype(z_ref.dtype)


def _s5(u, ops, d_skip, B, S):
    toep, bin_re, bin_im, cout_re, cout_im, a_re, a_im = ops
    G, W, N = bin_re.shape
    Lc, P = SSM_CHUNK, SSM_GROUP
    nc = S // Lc
    R = nc * B
    ut = u.reshape(B, nc, Lc, G, P).transpose(3, 1, 0, 2, 4).reshape(G, R, W)
    d_t = jnp.tile(d_skip.reshape(G, 1, P), (1, Lc, 1)).reshape(G, 1, W)
    per_g = lambda shape: pl.BlockSpec((1,) + shape, lambda g: (g, 0, 0))
    zt = pl.pallas_call(
        functools.partial(_s5_body, n_chunks=nc, batch=B),
        grid=(G,),
        in_specs=[per_g((R, W)), per_g((W, W)), per_g((W, N)), per_g((W, N)),
                  per_g((N, W)), per_g((N, W)), per_g((1, N)), per_g((1, N)), per_g((1, W))],
        out_specs=per_g((R, W)),
        out_shape=jax.ShapeDtypeStruct((G, R, W), BF16),
        scratch_shapes=[pltpu.VMEM((R, N), F32)] * 4,
        compiler_params=_cparams(1, R * W * 32 + R * N * 32),
        name="s5_chunked_scan",
    )(ut, toep, bin_re, bin_im, cout_re, cout_im, a_re, a_im, d_t)
    return zt.reshape(G, nc, B, Lc, P).transpose(2, 1, 3, 0, 4).reshape(B * S, G * P)


def _ffn_act_body(g_ref, v_ref, w_ref, b_ref, o_ref):
    g = g_ref[...].astype(F32)
    row = lax.broadcasted_iota(jnp.int32, g.shape, 0)
    y = w_ref[FFN_CONV - 1:FFN_CONV, :] * g + b_ref[...]
    for back in range(1, FFN_CONV):
        shifted = jnp.where(row >= back, pltpu.roll(g, back, axis=0), 0.0)
        y = y + w_ref[FFN_CONV - 1 - back:FFN_CONV - back, :] * shifted
    o_ref[...] = (_gelu(y) * v_ref[...].astype(F32)).astype(o_ref.dtype)


def _ffn_act(gv, conv_w, conv_b, B, S):
    T, F2 = gv.shape
    Fd = F2 // 2
    tc = _pick_tile(Fd, 256, LANE)
    nj = Fd // tc
    return pl.pallas_call(
        _ffn_act_body,
        grid=(B, nj),
        in_specs=[pl.BlockSpec((S, tc), lambda b, j: (b, j)),
                  pl.BlockSpec((S, tc), lambda b, j: (b, nj + j)),
                  pl.BlockSpec((FFN_CONV, tc), lambda b, j: (0, j)),
                  pl.BlockSpec((1, tc), lambda b, j: (0, j))],
        out_specs=pl.BlockSpec((S, tc), lambda b, j: (b, j)),
        out_shape=jax.ShapeDtypeStruct((T, Fd), BF16),
        compiler_params=_cparams(2, S * tc * 48),
        name="convglu_act",
    )(gv, gv, conv_w, conv_b.reshape(1, Fd))


def _wkv_body(*refs, n_chunks, n_pairs, has_vmix):
    C = WKV_CHUNK
    nd = RWKV_HEAD_DIM
    lanes = 2 * nd
    n_seq = 8 if has_vmix else 6
    r_ref, w_ref, k_ref, v_ref, a_ref, g_ref = refs[:6]
    vf_ref, vg_ref = refs[6:8] if has_vmix else (None, None)
    kk_ref, ka_ref, rk_ref, lnw_ref, lnb_ref, o_ref = refs[n_seq:n_seq + 6]
    (st_ref, inv_s, t_s, arbk_s, left_s, dk_s, v_s, v2_s, bonus_s, dec_s, u_s, ls_s) = refs[n_seq + 6:]
    unroll = 2 if n_chunks % 2 == 0 else 1

    @pl.when(pl.program_id(2) == 0)
    def _():
        st_ref[...] = jnp.zeros_like(st_ref)

    lane_r = lax.broadcasted_iota(jnp.int32, (lanes, lanes), 0)
    lane_c = lax.broadcasted_iota(jnp.int32, (lanes, lanes), 1)
    same_head = (lane_r // nd) == (lane_c // nd)
    head_ones = same_head.astype(BF16)
    pos_r = lax.broadcasted_iota(jnp.int32, (C, lanes), 0)
    pos_c = lax.broadcasted_iota(jnp.int32, (C, lanes), 1) % C
    strict = pos_r > pos_c
    incl = pos_r >= pos_c
    eye = (pos_r == pos_c).astype(F32)
    tri = (lax.broadcasted_iota(jnp.int32, (C, C), 0)
           >= lax.broadcasted_iota(jnp.int32, (C, C), 1)).astype(BF16)
    lane1 = lax.broadcasted_iota(jnp.int32, (1, lanes), 1)
    m0 = (lane1 < nd).astype(F32)
    m1 = 1.0 - m0

    def split(x):
        hi = x.astype(BF16)
        return hi, (x - hi.astype(F32)).astype(BF16)

    def head_sum(x):
        n = x.shape[0]
        z = jnp.dot(jnp.concatenate(split(x), axis=0), head_ones, preferred_element_type=F32)
        return z[:n] + z[n:]

    def stack(x):
        return jnp.concatenate([x * m0, x * m1], axis=0).astype(BF16)

    def bdot(a, b, dims=None):
        a, b = a.astype(BF16), b.astype(BF16)
        if dims is None:
            return jnp.dot(a, b, preferred_element_type=F32)
        return lax.dot_general(a, b, dims, preferred_element_type=F32)

    def problems(i):
        out = []
        for cc in range(unroll):
            c = i * unroll + cc
            for p in range(n_pairs):
                out.append((c * n_pairs + p, pl.ds(pl.multiple_of(c * C, C), C),
                            slice(p * lanes, (p + 1) * lanes), p))
        return out

    def pass_a(i, _):
        prob = problems(i)
        ld = lambda ref: [ref[rows, cols] for _, rows, cols, _ in prob]
        par = lambda ref: [ref[:, cols] for _, _, cols, _ in prob]
        n = range(len(prob))
        r, k, v, a, wpre = ld(r_ref), ld(k_ref), ld(v_ref), ld(a_ref), ld(w_ref)
        k_k, k_a, r_k = par(kk_ref), par(ka_ref), par(rk_ref)
        if has_vmix:
            vf, vg = ld(vf_ref), ld(vg_ref)
            v = [v[j] + (vf[j] - v[j]) * vg[j] for j in n]
        dlog = [-jnp.exp(-(jnp.maximum(-w, 0.0) + jnp.log(1.0 + jnp.exp(-jnp.abs(w)))) - 0.5)
                for w in wpre]
        kk = [k[j] * k_k[j] for j in n]
        k2 = [k[j] * (1.0 + (a[j] - 1.0) * k_a[j]) for j in n]
        sums = [head_sum(jnp.concatenate([kk[j] * kk[j], r[j] * k2[j] * r_k[j]], axis=0)) for j in n]
        z = [jnp.dot(tri, jnp.concatenate(split(dlog[j]), axis=1), preferred_element_type=F32)
             for j in n]
        lg = [zz[:, :lanes] + zz[:, lanes:] for zz in z]
        kk = [kk[j] / jnp.maximum(jnp.sqrt(sums[j][:C]), 1e-12) for j in n]
        bv = [kk[j] * a[j] for j in n]
        e_neg = [jnp.exp(-l) for l in lg]
        e_end = [jnp.exp(l[C - 1:C, :] - l) for l in lg]
        left = [jnp.concatenate([-kk[j] * jnp.exp(lg[j] - dlog[j]), r[j] * jnp.exp(lg[j])],
                                axis=0).astype(BF16) for j in n]
        right = [jnp.concatenate([stack(bv[j] * e_neg[j]), stack(k2[j] * e_neg[j])], axis=0)
                 for j in n]
        aa = [bdot(left[j], right[j], NT_DIMS) for j in n]
        a_ab = [jnp.where(strict, m[:C, :lanes], 0.0) for m in aa]
        v2 = [stack(x) for x in v]
        t = [bdot(jnp.where(strict, aa[j][:C, lanes:], 0.0), v2[j]) for j in n]
        power = [bdot(m, stack(m)) for m in a_ab]
        inv = [eye + m for m in a_ab]
        for _ in range(2, int(math.log2(C))):
            z = [bdot(jnp.concatenate([inv[j], power[j]], axis=0), stack(power[j])) for j in n]
            inv = [inv[j] + z[j][:C] for j in n]
            power = [zz[C:] for zz in z]
        inv = [inv[j] + bdot(inv[j], stack(power[j])) for j in n]
        for j, (slot, rows, cols, _) in enumerate(prob):
            inv_s[slot] = inv[j].astype(BF16)
            t_s[slot] = t[j]
            arbk_s[slot] = jnp.where(jnp.concatenate([incl, incl], axis=1), aa[j][C:], 0.0).astype(BF16)
            left_s[slot] = left[j]
            dk_s[slot] = jnp.concatenate([bv[j] * e_end[j], k2[j] * e_end[j]], axis=0).astype(BF16)
            v_s[slot] = v[j].astype(BF16)
            v2_s[slot] = v2[j]
            bonus_s[slot] = sums[j][C:] * v[j]
            dec_s[slot] = jnp.exp(lg[j][C - 1:C, :])
        return 0

    def pass_b(c, _):
        n = range(n_pairs)
        slot = [c * n_pairs + p for p in n]
        state = [st_ref[p] for p in n]
        ls = [bdot(left_s[slot[p]], state[p], NT_DIMS) for p in n]
        u = [bdot(inv_s[slot[p]], stack(ls[p][:C] + t_s[slot[p]])) for p in n]
        u = [x.astype(BF16) for x in u]
        upd = [lax.dot_general(jnp.concatenate([u[p], v_s[slot[p]]], axis=0), dk_s[slot[p]], TN_DIMS,
                               preferred_element_type=F32) for p in n]
        for p in n:
            st_ref[p] = state[p] * dec_s[slot[p]] + jnp.where(same_head, upd[p], 0.0)
            u_s[slot[p]] = u[p]
            ls_s[slot[p]] = ls[p][C:]
        return 0

    def pass_c(i, _):
        prob = problems(i)
        n = range(len(prob))
        uv = [jnp.concatenate([stack(u_s[slot].astype(F32)), v2_s[slot]], axis=0)
              for slot, _, _, _ in prob]
        y = [ls_s[prob[j][0]] + jnp.dot(arbk_s[prob[j][0]], uv[j], preferred_element_type=F32)
             for j in n]
        mean = [head_sum(x) * np.float32(1.0 / nd) for x in y]
        d = [y[j] - mean[j] for j in n]
        var = [head_sum(x * x) * np.float32(1.0 / nd) for x in d]
        for j, (slot, rows, cols, _) in enumerate(prob):
            yn = d[j] * lax.rsqrt(var[j] + RWKV_GN_EPS) * lnw_ref[:, cols] + lnb_ref[:, cols]
            o_ref[rows, cols] = ((yn + bonus_s[slot]) * g_ref[rows, cols].astype(F32)).astype(o_ref.dtype)
        return 0

    lax.fori_loop(0, n_chunks // unroll, pass_a, 0)
    lax.fori_loop(0, n_chunks, pass_b, 0)
    lax.fori_loop(0, n_chunks // unroll, pass_c, 0)


def _wkv(r, wpre, k, v, a, g, v_first, v_gate, k_k, k_a, r_k, ln_w, ln_b, B, S):
    T, D = r.shape
    lanes = 2 * RWKV_HEAD_DIM
    has_vmix = v_first is not None
    n_pairs = 4 if D % (4 * lanes) == 0 else 1
    width = n_pairs * lanes
    ts = _pick_tile(S, 512, WKV_CHUNK)
    nt = S // ts
    seq = pl.BlockSpec((ts, width), lambda b, h, t: (b * nt + t, h))
    par = pl.BlockSpec((1, width), lambda b, h, t: (0, h))
    args = [r, wpre, k, v, a, g] + ([v_first, v_gate] if has_vmix else [])
    params = [p.reshape(1, D) for p in (k_k, k_a, r_k, ln_w, ln_b)]
    C = WKV_CHUNK
    n_chunks = ts // C
    slots = n_chunks * n_pairs
    scratch = [pltpu.VMEM((n_pairs, lanes, lanes), F32),
               pltpu.VMEM((slots, C, lanes), BF16),
               pltpu.VMEM((slots, C, lanes), F32),
               pltpu.VMEM((slots, C, 2 * lanes), BF16),
               pltpu.VMEM((slots, 2 * C, lanes), BF16),
               pltpu.VMEM((slots, 2 * C, lanes), BF16),
               pltpu.VMEM((slots, C, lanes), BF16),
               pltpu.VMEM((slots, 2 * C, lanes), BF16),
               pltpu.VMEM((slots, C, lanes), F32),
               pltpu.VMEM((slots, 1, lanes), F32),
               pltpu.VMEM((slots, C, lanes), BF16),
               pltpu.VMEM((slots, C, lanes), F32)]
    scratch_bytes = slots * C * lanes * 30
    return pl.pallas_call(
        functools.partial(_wkv_body, n_chunks=n_chunks, n_pairs=n_pairs, has_vmix=has_vmix),
        grid=(B, D // width, nt),
        in_specs=[seq] * len(args) + [par] * len(params),
        out_specs=seq,
        out_shape=jax.ShapeDtypeStruct((T, D), BF16),
        scratch_shapes=scratch,
        compiler_params=_cparams(3, 2 * ts * width * 4 * (len(args) + 1) + scratch_bytes
                                 + 16 * 1024 * 1024),
        name="wkv7_chunked",
    )(*args, *params)


def _pad_to(x, n, axis):
    pad = n - x.shape[axis]
    if pad == 0:
        return x
    widths = [(0, 0)] * x.ndim
    widths[axis] = (0, pad)
    return jnp.pad(x, widths)


def _lora(x, w_down, w_up, *, bias=None, act_mid=None, act_out=None, out_dtype=F32, name="lora"):
    inner = -(-w_down.shape[1] // LANE) * LANE
    wd = _pad_to(w_down, inner, 1).astype(BF16)
    wu = _pad_to(w_up, inner, 0).astype(BF16)
    mid = _matmul([(x, wd)], out_dtype=BF16, act=act_mid, name=name + "_down")
    return _matmul([(mid, wu)], out_dtype=out_dtype, bias=bias, act=act_out, name=name + "_up")


def _hybrid_layer(x, B, S, norm, w_in, q_norm, k_norm, lam_re, lam_im, log_dt, b_re, b_im, c_re, c_im,
                  d_skip, glu_w, glu_b, w_out):
    aw = w_out.shape[0] - d_skip.shape[0]
    n_heads = aw // ATTN_HEAD_DIM
    h = _rmsnorm(x, norm)
    qkvu = _matmul([(h, w_in.astype(BF16))], out_dtype=BF16, name="hybrid_in_proj")
    y_attn = _attention(qkvu, q_norm, k_norm, B, S, n_heads)
    ops = _s5_chunk_operators(lam_re, lam_im, log_dt, b_re, b_im, c_re, c_im)
    z = _s5(qkvu[:, 3 * aw:], ops, d_skip, B, S)
    y_ssm = _matmul([(z, glu_w.astype(BF16))], out_dtype=BF16, bias=glu_b, act="sigmoid", mul=z,
                    name="ssm_glu")
    w_out = w_out.astype(BF16)
    return _matmul([(y_attn, w_out[:aw]), (y_ssm, w_out[aw:])], out_dtype=F32, residual=x,
                   name="hybrid_out_proj")


def _rwkv_layer(x, B, S, norm, mu, w_r, w_k, w_v, w0, w1, w2, a0, a1, a2, g1, g2, k_k, k_a, r_k,
                ln_w, ln_b, w_o, v_first, v_mix):
    l_r, l_w, l_k, l_v, l_a, l_g = _rwkv_lerp(x, norm, mu, S)
    r = _matmul([(l_r, w_r.astype(BF16))], out_dtype=F32, name="rwkv_r")
    k = _matmul([(l_k, w_k.astype(BF16))], out_dtype=F32, name="rwkv_k")
    v = _matmul([(l_v, w_v.astype(BF16))], out_dtype=F32, name="rwkv_v")
    wpre = _lora(l_w, w1, w2, bias=w0, act_mid="tanh", name="rwkv_decay")
    a = _lora(l_a, a1, a2, bias=a0, act_out="sigmoid", name="rwkv_iclr")
    g = _lora(l_g, g1, g2, act_mid="sigmoid", out_dtype=BF16, name="rwkv_gate")
    if v_mix is None:
        v_gate, vf = None, None
        v_first = v
    else:
        v0, v1, v2 = v_mix
        v_gate = _lora(l_v, v1, v2, bias=v0, act_out="sigmoid", name="rwkv_vmix")
        vf = v_first
    y = _wkv(r, wpre, k, v, a, g, vf, v_gate, k_k, k_a, r_k.reshape(-1), ln_w, ln_b, B, S)
    return _matmul([(y, w_o.astype(BF16))], out_dtype=F32, residual=x, name="rwkv_out_proj"), v_first


def _ffn_layer(x, B, S, norm, w_up, conv_w, conv_b, w_down):
    h = _rmsnorm(x, norm)
    gv = _matmul([(h, w_up.astype(BF16))], out_dtype=BF16, name="ffn_up")
    act = _ffn_act(gv, conv_w, conv_b, B, S)
    return _matmul([(act, w_down.astype(BF16))], out_dtype=F32, residual=x, name="ffn_down")


def kernel(x, mix_norm, hy_w_in, attn_q_norm, attn_k_norm, ssm_lambda_re, ssm_lambda_im, ssm_log_dt, ssm_b_re, ssm_b_im, ssm_c_re, ssm_c_im, ssm_d, ssm_glu_w, ssm_glu_b, hy_w_out, rwkv_mu, rwkv_w_r, rwkv_w_k, rwkv_w_v, rwkv_w0, rwkv_w1, rwkv_w2, rwkv_a0, rwkv_a1, rwkv_a2, rwkv_g1, rwkv_g2, rwkv_k_k, rwkv_k_a, rwkv_r_k, rwkv_ln_w, rwkv_ln_b, rwkv_w_o, rwkv_v0, rwkv_v1, rwkv_v2, ffn_norm, ffn_w_up, ffn_conv_w, ffn_conv_b, ffn_w_down):
    B, S, D = x.shape
    depth = mix_norm.shape[0]
    x = x.reshape(B * S, D)
    v_first = None
    for layer in range(depth):
        i = layer // 2
        if layer % 2 == 0:
            x = _hybrid_layer(x, B, S, mix_norm[layer], hy_w_in[i], attn_q_norm[i], attn_k_norm[i],
                              ssm_lambda_re[i], ssm_lambda_im[i], ssm_log_dt[i], ssm_b_re[i],
                              ssm_b_im[i], ssm_c_re[i], ssm_c_im[i], ssm_d[i], ssm_glu_w[i],
                              ssm_glu_b[i], hy_w_out[i])
        else:
            v_mix = None if i == 0 else (rwkv_v0[i - 1], rwkv_v1[i - 1], rwkv_v2[i - 1])
            x, v_first = _rwkv_layer(x, B, S, mix_norm[layer], rwkv_mu[i], rwkv_w_r[i], rwkv_w_k[i],
                                     rwkv_w_v[i], rwkv_w0[i], rwkv_w1[i], rwkv_w2[i], rwkv_a0[i],
                                     rwkv_a1[i], rwkv_a2[i], rwkv_g1[i], rwkv_g2[i], rwkv_k_k[i],
                                     rwkv_k_a[i], rwkv_r_k[i], rwkv_ln_w[i], rwkv_ln_b[i],
                                     rwkv_w_o[i], v_first, v_mix)
        x = _ffn_layer(x, B, S, ffn_norm[layer], ffn_w_up[layer], ffn_conv_w[layer],
                       ffn_conv_b[layer], ffn_w_down[layer])
    return x.reshape(B, S, D)
```

```python
import functools
import math

import numpy as np
import jax
import jax.numpy as jnp
from jax import lax
from jax.experimental import pallas as pl
from jax.experimental.pallas import tpu as pltpu

F32 = jnp.float32
BF16 = jnp.bfloat16
HIGHEST = lax.Precision.HIGHEST

RMS_EPS = 1e-6
ATTN_HEAD_DIM = 128
DILATED_BRANCHES = ((128, 1), (512, 4), (2048, 16))
ROPE_THETA = 10000.0
SSM_GROUP = 16
SSM_STATE = 64
SSM_CHUNK = 16
RWKV_HEAD_DIM = 64
RWKV_GN_EPS = 64e-5
WKV_CHUNK = 64
FFN_CONV = 3

V7X_VMEM_BYTES = 64 * 1024 * 1024
VMEM_BUDGET = 48 * 1024 * 1024
LANE = 128
SUBLANE = 8

NT_DIMS = (((1,), (1,)), ((), ()))
TN_DIMS = (((0,), (0,)), ((), ()))


def _cparams(n_grid, vmem_bytes):
    limit = int(min(max(vmem_bytes, 16 * 1024 * 1024), V7X_VMEM_BYTES - 6 * 1024 * 1024))
    return pltpu.CompilerParams(dimension_semantics=("arbitrary",) * n_grid,
                                vmem_limit_bytes=limit)


def _pick_tile(n, pref, align):
    t = min(pref, n)
    t -= t % align
    while t >= align:
        if n % t == 0:
            return t
        t -= align
    return n


def _sigmoid(x):
    return 1.0 / (1.0 + jnp.exp(-x))


def _gelu(x):
    return 0.5 * x * (1.0 + lax.erf(x * np.float32(math.sqrt(0.5))))


def _rms(x, gain):
    return x * lax.rsqrt(jnp.mean(x * x, axis=-1, keepdims=True) + RMS_EPS) * gain


def _rmsnorm_body(x_ref, g_ref, o_ref):
    o_ref[...] = _rms(x_ref[...], g_ref[...]).astype(o_ref.dtype)


def _rmsnorm(x, gain, out_dtype=BF16):
    T, D = x.shape
    tm = _pick_tile(T, 256, SUBLANE)
    return pl.pallas_call(
        _rmsnorm_body,
        grid=(T // tm,),
        in_specs=[pl.BlockSpec((tm, D), lambda i: (i, 0)),
                  pl.BlockSpec((1, D), lambda i: (0, 0))],
        out_specs=pl.BlockSpec((tm, D), lambda i: (i, 0)),
        out_shape=jax.ShapeDtypeStruct((T, D), out_dtype),
        compiler_params=_cparams(1, 6 * tm * D * 4),
        name="rmsnorm",
    )(x, gain.reshape(1, D))


def _lerp_body(x_ref, xh_ref, g_ref, mu_ref, *o_refs, tm, seq):
    i = pl.program_id(0)
    g = g_ref[...]
    h = _rms(x_ref[...], g)
    hh = _rms(xh_ref[...], g)
    prev = jnp.where((i * tm) % seq == 0, 0.0, hh[SUBLANE - 1:SUBLANE, :])
    hs = pltpu.roll(h, 1, axis=0)
    row = lax.broadcasted_iota(jnp.int32, h.shape, 0)
    xx = jnp.where(row == 0, prev, hs) - h
    for n, o_ref in enumerate(o_refs):
        o_ref[...] = (h + xx * mu_ref[n:n + 1, :]).astype(o_ref.dtype)


def _rwkv_lerp(x, gain, mu, seq):
    T, D = x.shape
    n = mu.shape[0]
    tm = _pick_tile(seq, 128, SUBLANE)
    hb = tm // SUBLANE
    return pl.pallas_call(
        functools.partial(_lerp_body, tm=tm, seq=seq),
        grid=(T // tm,),
        in_specs=[pl.BlockSpec((tm, D), lambda i: (i, 0)),
                  pl.BlockSpec((SUBLANE, D), lambda i: (jnp.maximum(i * hb - 1, 0), 0)),
                  pl.BlockSpec((1, D), lambda i: (0, 0)),
                  pl.BlockSpec((n, D), lambda i: (0, 0))],
        out_specs=[pl.BlockSpec((tm, D), lambda i: (i, 0))] * n,
        out_shape=[jax.ShapeDtypeStruct((T, D), BF16)] * n,
        compiler_params=_cparams(1, tm * D * (2 * 4 + 2 * 2 * n + 4 * 4)),
        name="rwkv_lerp",
    )(x, x, gain.reshape(1, D), mu)


def _mm_body(*refs, n_pairs, has_bias, act, has_mul, has_res):
    acc = None
    idx = 0
    for _ in range(n_pairs):
        d = jnp.dot(refs[idx][...], refs[idx + 1][...], preferred_element_type=F32)
        acc = d if acc is None else acc + d
        idx += 2
    if has_bias:
        acc = acc + refs[idx][...]
        idx += 1
    if act == "tanh":
        acc = jnp.tanh(acc)
    elif act == "sigmoid":
        acc = _sigmoid(acc)
    if has_mul:
        acc = acc * refs[idx][...].astype(F32)
        idx += 1
    if has_res:
        acc = acc + refs[idx][...]
        idx += 1
    o_ref = refs[idx]
    o_ref[...] = acc.astype(o_ref.dtype)


def _mm_tiles(M, N, k_total, out_bytes, extra_bytes):
    for tm_pref, tn_pref in ((1024, 512), (512, 512), (512, 256), (256, 256), (128, 256), (128, 128)):
        tm = _pick_tile(M, tm_pref, 16)
        tn = _pick_tile(N, tn_pref, LANE)
        need = (2 * tm * k_total * 2 + 2 * k_total * tn * 2
                + tm * tn * (2 * out_bytes + 2 * extra_bytes + 8))
        if need <= VMEM_BUDGET:
            return tm, tn, need
    return tm, tn, need


def _matmul(pairs, *, out_dtype, bias=None, act=None, mul=None, residual=None, name="matmul"):
    M = pairs[0][0].shape[0]
    N = pairs[0][1].shape[-1]
    k_total = sum(p[0].shape[1] for p in pairs)
    out_bytes = jnp.dtype(out_dtype).itemsize
    extra = (2 if mul is not None else 0) + (4 if residual is not None else 0)
    tm, tn, need = _mm_tiles(M, N, k_total, out_bytes, extra)
    args, in_specs = [], []
    for pair in pairs:
        a, w = pair[0], pair[1]
        layer, kblock = (pair[2], pair[3]) if len(pair) == 4 else (0, 0)
        if w.ndim == 2:
            w = w[None]
        K = a.shape[1]
        assert a.dtype == BF16 and w.dtype == BF16 and a.shape[0] == M
        assert w.shape[2] == N and w.shape[1] % K == 0
        args += [a, w]
        in_specs += [pl.BlockSpec((tm, K), lambda i, j: (i, 0)),
                     pl.BlockSpec((None, K, tn), lambda i, j, layer=layer, kblock=kblock: (layer, kblock, j))]
    if bias is not None:
        args.append(bias.reshape(1, N).astype(F32))
        in_specs.append(pl.BlockSpec((1, tn), lambda i, j: (0, j)))
    if mul is not None:
        args.append(mul)
        in_specs.append(pl.BlockSpec((tm, tn), lambda i, j: (i, j)))
    if residual is not None:
        args.append(residual)
        in_specs.append(pl.BlockSpec((tm, tn), lambda i, j: (i, j)))
    return pl.pallas_call(
        functools.partial(_mm_body, n_pairs=len(pairs), has_bias=bias is not None, act=act,
                          has_mul=mul is not None, has_res=residual is not None),
        grid=(M // tm, N // tn),
        in_specs=in_specs,
        out_specs=pl.BlockSpec((tm, tn), lambda i, j: (i, j)),
        out_shape=jax.ShapeDtypeStruct((M, N), out_dtype),
        compiler_params=_cparams(2, need + 8 * 1024 * 1024),
        name=name,
    )(*args)


def _attn_bias(S, tq):
    delta = (S - tq) + np.arange(tq)[:, None] - np.arange(S)[None, :]
    mult = np.zeros(delta.shape, np.float64)
    for window, dilation in DILATED_BRANCHES:
        mult += (delta >= 0) & (delta <= window) & (delta % dilation == 0)
    bias = np.where(mult > 0, np.log(np.maximum(mult, 1.0)), -1e30)
    return jnp.asarray(bias, F32)


def _rope_tables(S):
    half = ATTN_HEAD_DIM // 2
    inv_freq = ROPE_THETA ** (-jnp.arange(half, dtype=F32) / half)
    ang = jnp.arange(S, dtype=jnp.int32).astype(F32)[:, None] * inv_freq[None, :]
    cos, sin = jnp.cos(ang), jnp.sin(ang)
    return jnp.concatenate([cos, cos], axis=-1), jnp.concatenate([-sin, sin], axis=-1)


def _attn_body(q_ref, k_ref, v_ref, qg_ref, kg_ref, cos_ref, sin_ref, bias_ref, o_ref,
               qs_ref, ks_ref, *, S, tq):
    half = ATTN_HEAD_DIM // 2

    def prep(x_ref, g_ref, scale):
        x = _rms(x_ref[...].astype(F32), g_ref[...])
        x = x * cos_ref[...] + pltpu.roll(x, half, axis=1) * sin_ref[...]
        return (x * scale).astype(BF16)

    qs_ref[...] = prep(q_ref, qg_ref, np.float32(ATTN_HEAD_DIM ** -0.5))
    ks_ref[...] = prep(k_ref, kg_ref, np.float32(1.0))

    n_blocks = S // tq

    def scores(i):
        return lax.dot_general(qs_ref[i * tq:(i + 1) * tq, :], ks_ref[:(i + 1) * tq, :], NT_DIMS,
                               preferred_element_type=F32)

    s_next = scores(0)
    for i in range(n_blocks):
        s = s_next + bias_ref[:, (n_blocks - 1 - i) * tq:]
        if i + 1 < n_blocks:
            s_next = scores(i + 1)
        p = jnp.exp(s - jnp.max(s, axis=-1, keepdims=True))
        l = jnp.sum(p, axis=-1, keepdims=True)
        acc = jnp.dot(p.astype(BF16), v_ref[:(i + 1) * tq, :], preferred_element_type=F32)
        o_ref[i * tq:(i + 1) * tq, :] = (acc / l).astype(o_ref.dtype)


def _attention(qkvu, q_gain, k_gain, B, S, n_heads):
    T = B * S
    dh = ATTN_HEAD_DIM
    tq = _pick_tile(S, 256, LANE)
    cos, sin = _rope_tables(S)
    bias = _attn_bias(S, tq)
    head = lambda off: pl.BlockSpec((S, dh), lambda b, h: (b, off + h))
    const2 = lambda shape: pl.BlockSpec(shape, lambda b, h: (0, 0))
    return pl.pallas_call(
        functools.partial(_attn_body, S=S, tq=tq),
        grid=(B, n_heads),
        in_specs=[head(0), head(n_heads), head(2 * n_heads),
                  const2((1, dh)), const2((1, dh)), const2((S, dh)), const2((S, dh)),
                  const2(bias.shape)],
        out_specs=pl.BlockSpec((S, dh), lambda b, h: (b, h)),
        out_shape=jax.ShapeDtypeStruct((T, n_heads * dh), BF16),
        scratch_shapes=[pltpu.VMEM((S, dh), BF16), pltpu.VMEM((S, dh), BF16)],
        compiler_params=_cparams(2, 2 * bias.size * 4 + 24 * S * dh * 4),
        name="dilated_attention",
    )(qkvu, qkvu, qkvu, q_gain.reshape(1, dh), k_gain.reshape(1, dh), cos, sin, bias)


def _s5_chunk_operators(lam_re, lam_im, log_dt, b_re, b_im, c_re, c_im):
    Lc = SSM_CHUNK
    G, N = lam_re.shape
    P = b_re.shape[-1]
    dt = jnp.exp(log_dt)[:, None]
    mag = jnp.exp(lam_re * dt)
    ab_re, ab_im = mag * jnp.cos(lam_im * dt), mag * jnp.sin(lam_im * dt)
    den = lam_re * lam_re + lam_im * lam_im
    nr, ni = ab_re - 1.0, ab_im
    coef_re = (nr * lam_re + ni * lam_im) / den
    coef_im = (ni * lam_re - nr * lam_im) / den
    bb_re = coef_re[..., None] * b_re - coef_im[..., None] * b_im
    bb_im = coef_re[..., None] * b_im + coef_im[..., None] * b_re
    steps = jnp.arange(Lc + 1, dtype=F32)[:, None, None]
    pmag = jnp.exp(lam_re * dt * steps)
    pw_re, pw_im = pmag * jnp.cos(lam_im * dt * steps), pmag * jnp.sin(lam_im * dt * steps)
    ca_re = c_re[None] * pw_re[:Lc, :, None, :] - c_im[None] * pw_im[:Lc, :, None, :]
    ca_im = c_re[None] * pw_im[:Lc, :, None, :] + c_im[None] * pw_re[:Lc, :, None, :]
    lag = (jnp.sum(ca_re[..., None] * bb_re[None, :, None], axis=3)
           - jnp.sum(ca_im[..., None] * bb_im[None, :, None], axis=3))
    ti = jnp.arange(Lc)[:, None]
    to = jnp.arange(Lc)[None, :]
    sel = jnp.clip(to - ti, 0, Lc - 1)
    toep = jnp.where((to >= ti)[:, :, None, None, None], lag[sel], 0.0)
    toep = toep.transpose(2, 0, 4, 1, 3).reshape(G, Lc * P, Lc * P)
    rev = pw_re[Lc - 1 - jnp.arange(Lc)], pw_im[Lc - 1 - jnp.arange(Lc)]
    bin_re = rev[0][..., None] * bb_re[None] - rev[1][..., None] * bb_im[None]
    bin_im = rev[0][..., None] * bb_im[None] + rev[1][..., None] * bb_re[None]
    bin_re = bin_re.transpose(1, 0, 3, 2).reshape(G, Lc * P, N)
    bin_im = bin_im.transpose(1, 0, 3, 2).reshape(G, Lc * P, N)
    nxt_re, nxt_im = pw_re[1:], pw_im[1:]
    co_re = c_re[None] * nxt_re[:, :, None, :] - c_im[None] * nxt_im[:, :, None, :]
    co_im = c_re[None] * nxt_im[:, :, None, :] + c_im[None] * nxt_re[:, :, None, :]
    cout_re = co_re.transpose(1, 3, 0, 2).reshape(G, N, Lc * P)
    cout_im = (-co_im).transpose(1, 3, 0, 2).reshape(G, N, Lc * P)
    return (toep.astype(BF16), bin_re.astype(BF16), bin_im.astype(BF16),
            cout_re.astype(BF16), cout_im.astype(BF16),
            pw_re[Lc][:, None, :], pw_im[Lc][:, None, :])


def _s5_body(u_ref, toep_ref, bre_ref, bim_ref, cre_ref, cim_ref, are_ref, aim_ref, d_ref, z_ref,
             xin_re, xin_im, xst_re, xst_im, *, n_chunks, batch):
    u = u_ref[0]
    xin_re[...] = jnp.dot(u, bre_ref[0], preferred_element_type=F32)
    xin_im[...] = jnp.dot(u, bim_ref[0], preferred_element_type=F32)
    a_re, a_im = are_ref[0], aim_ref[0]

    def step(c, carry):
        x_re, x_im = carry
        rows = pl.ds(pl.multiple_of(c * batch, batch), batch)
        xst_re[rows, :] = x_re
        xst_im[rows, :] = x_im
        return (a_re * x_re - a_im * x_im + xin_re[rows, :],
                a_re * x_im + a_im * x_re + xin_im[rows, :])

    zero = jnp.zeros((batch, xin_re.shape[1]), F32)
    lax.fori_loop(0, n_chunks, step, (zero, zero))
    y = (jnp.dot(u, toep_ref[0], preferred_element_type=F32)
         + jnp.dot(xst_re[...].astype(BF16), cre_ref[0], preferred_element_type=F32)
         + jnp.dot(xst_im[...].astype(BF16), cim_ref[0], preferred_element_type=F32))
    y = y + d_ref[0] * u.astype(F32)
    z_ref[0] = _gelu(y).astype(z_ref.dtype)


def _s5(u, ops, d_skip, B, S):
    toep, bin_re, bin_im, cout_re, cout_im, a_re, a_im = ops
    G, W, N = bin_re.shape
    Lc, P = SSM_CHUNK, SSM_GROUP
    nc = S // Lc
    R = nc * B
    ut = u.reshape(B, nc, Lc, G, P).transpose(3, 1, 0, 2, 4).reshape(G, R, W)
    d_t = jnp.tile(d_skip.reshape(G, 1, P), (1, Lc, 1)).reshape(G, 1, W)
    per_g = lambda shape: pl.BlockSpec((1,) + shape, lambda g: (g, 0, 0))
    zt = pl.pallas_call(
        functools.partial(_s5_body, n_chunks=nc, batch=B),
        grid=(G,),
        in_specs=[per_g((R, W)), per_g((W, W)), per_g((W, N)), per_g((W, N)),
                  per_g((N, W)), per_g((N, W)), per_g((1, N)), per_g((1, N)), per_g((1, W))],
        out_specs=per_g((R, W)),
        out_shape=jax.ShapeDtypeStruct((G, R, W), BF16),
        scratch_shapes=[pltpu.VMEM((R, N), F32)] * 4,
        compiler_params=_cparams(1, R * W * 32 + R * N * 32),
        name="s5_chunked_scan",
    )(ut, toep, bin_re, bin_im, cout_re, cout_im, a_re, a_im, d_t)
    return zt.reshape(G, nc, B, Lc, P).transpose(2, 1, 3, 0, 4).reshape(B * S, G * P)


def _ffn_up_body(h_ref, wg_ref, wv_ref, cw_ref, cb_ref, o_ref, gate_ref, val_ref, tail_ref,
                 *, tm, seq, nj):
    i, j = pl.program_id(0), pl.program_id(1)

    def conv(g, shifted):
        y = cw_ref[FFN_CONV - 1:FFN_CONV, :] * g + cb_ref[...]
        for back in range(1, FFN_CONV):
            y = y + cw_ref[FFN_CONV - 1 - back:FFN_CONV - back, :] * shifted(back)
        return y

    def activate():
        jj = j - 1
        gate, val = gate_ref[...], val_ref[...]
        y = conv(gate, lambda back: pltpu.roll(gate, back, axis=0))
        o_ref[...] = (_gelu(y) * val).astype(o_ref.dtype)
        prev = jnp.where((i * tm) % seq == 0, 0.0, tail_ref[jj])
        top = gate[:SUBLANE, :]
        row = lax.broadcasted_iota(jnp.int32, top.shape, 0)
        y_top = conv(top, lambda back: jnp.where(row < back, pltpu.roll(prev, back, axis=0),
                                                 pltpu.roll(top, back, axis=0)))
        o_ref[:SUBLANE, :] = (_gelu(y_top) * val[:SUBLANE, :]).astype(o_ref.dtype)
        tail_ref[jj] = gate[tm - SUBLANE:, :]

    def multiply():
        h = h_ref[...]
        gate_ref[...] = jnp.dot(h, wg_ref[...], preferred_element_type=F32)
        val_ref[...] = jnp.dot(h, wv_ref[...], preferred_element_type=F32)

    @pl.when(jnp.logical_and(i == 0, j == 0))
    def _():
        tail_ref[...] = jnp.zeros_like(tail_ref)

    @pl.when(j == 0)
    def _():
        multiply()

    @pl.when(jnp.logical_and(j > 0, j < nj))
    def _():
        activate()
        multiply()

    @pl.when(j == nj)
    def _():
        activate()


def _ffn_up_act(h, w_up, layer, conv_w, conv_b, seq):
    T, D = h.shape
    Fd = w_up.shape[2] // 2
    tm = _pick_tile(seq, 1024, 16)
    tn = _pick_tile(Fd, 512, LANE)
    nj = Fd // tn
    cur = lambda j: jnp.minimum(j, nj - 1)
    lag = lambda j: jnp.maximum(j - 1, 0)
    need = 2 * tm * D * 2 + 4 * D * tn * 2 + 2 * tm * tn * 2 + 12 * tm * tn * 4
    return pl.pallas_call(
        functools.partial(_ffn_up_body, tm=tm, seq=seq, nj=nj),
        grid=(T // tm, nj + 1),
        in_specs=[pl.BlockSpec((tm, D), lambda i, j: (i, 0)),
                  pl.BlockSpec((None, D, tn), lambda i, j: (layer, 0, cur(j))),
                  pl.BlockSpec((None, D, tn), lambda i, j: (layer, 0, nj + cur(j))),
                  pl.BlockSpec((FFN_CONV, tn), lambda i, j: (0, lag(j))),
                  pl.BlockSpec((1, tn), lambda i, j: (0, lag(j)))],
        out_specs=pl.BlockSpec((tm, tn), lambda i, j: (i, lag(j))),
        out_shape=jax.ShapeDtypeStruct((T, Fd), BF16),
        scratch_shapes=[pltpu.VMEM((tm, tn), F32), pltpu.VMEM((tm, tn), F32),
                        pltpu.VMEM((nj, SUBLANE, tn), F32)],
        compiler_params=_cparams(2, need),
        name="ffn_up_convglu",
    )(h, w_up, w_up, conv_w, conv_b.reshape(1, Fd))


def _wkv_body(*refs, n_chunks, n_pairs, has_vmix):
    C = WKV_CHUNK
    nd = RWKV_HEAD_DIM
    lanes = 2 * nd
    n_seq = 8 if has_vmix else 6
    r_ref, w_ref, k_ref, v_ref, a_ref, g_ref = refs[:6]
    vf_ref, vg_ref = refs[6:8] if has_vmix else (None, None)
    kk_ref, ka_ref, rk_ref, lnw_ref, lnb_ref, o_ref = refs[n_seq:n_seq + 6]
    (st_ref, inv_s, t_s, arbk_s, left_s, dk_s, v_s, v2_s, bonus_s, dec_s, u_s, ls_s) = refs[n_seq + 6:]
    unroll = 2 if (n_chunks % 2 == 0 and n_pairs <= 4) else 1

    @pl.when(pl.program_id(2) == 0)
    def _():
        st_ref[...] = jnp.zeros_like(st_ref)

    lane_r = lax.broadcasted_iota(jnp.int32, (lanes, lanes), 0)
    lane_c = lax.broadcasted_iota(jnp.int32, (lanes, lanes), 1)
    same_head = (lane_r // nd) == (lane_c // nd)
    head_ones = same_head.astype(BF16)
    pos_r = lax.broadcasted_iota(jnp.int32, (C, lanes), 0)
    pos_c = lax.broadcasted_iota(jnp.int32, (C, lanes), 1) % C
    strict = pos_r > pos_c
    incl = pos_r >= pos_c
    eye = (pos_r == pos_c).astype(F32)
    tri = (lax.broadcasted_iota(jnp.int32, (C, C), 0)
           >= lax.broadcasted_iota(jnp.int32, (C, C), 1)).astype(BF16)
    lane1 = lax.broadcasted_iota(jnp.int32, (1, lanes), 1)
    m0 = (lane1 < nd).astype(F32)
    m1 = 1.0 - m0

    def split(x):
        hi = x.astype(BF16)
        return hi, (x - hi.astype(F32)).astype(BF16)

    def head_sum(x):
        n = x.shape[0]
        z = jnp.dot(jnp.concatenate(split(x), axis=0), head_ones, preferred_element_type=F32)
        return z[:n] + z[n:]

    def stack(x):
        return jnp.concatenate([x * m0, x * m1], axis=0).astype(BF16)

    def bdot(a, b, dims=None):
        a, b = a.astype(BF16), b.astype(BF16)
        if dims is None:
            return jnp.dot(a, b, preferred_element_type=F32)
        return lax.dot_general(a, b, dims, preferred_element_type=F32)

    def problems(i):
        out = []
        for cc in range(unroll):
            c = i * unroll + cc
            for p in range(n_pairs):
                out.append((c * n_pairs + p, pl.ds(pl.multiple_of(c * C, C), C),
                            slice(p * lanes, (p + 1) * lanes), p))
        return out

    def pass_a(i, _):
        prob = problems(i)
        ld = lambda ref: [ref[rows, cols] for _, rows, cols, _ in prob]
        par = lambda ref: [ref[:, cols] for _, _, cols, _ in prob]
        n = range(len(prob))
        r, k, v, a, wpre = ld(r_ref), ld(k_ref), ld(v_ref), ld(a_ref), ld(w_ref)
        k_k, k_a, r_k = par(kk_ref), par(ka_ref), par(rk_ref)
        if has_vmix:
            vf, vg = ld(vf_ref), ld(vg_ref)
            v = [v[j] + (vf[j] - v[j]) * vg[j] for j in n]
        dlog = [-jnp.exp(-(jnp.maximum(-w, 0.0) + jnp.log(1.0 + jnp.exp(-jnp.abs(w)))) - 0.5)
                for w in wpre]
        kk = [k[j] * k_k[j] for j in n]
        k2 = [k[j] * (1.0 + (a[j] - 1.0) * k_a[j]) for j in n]
        sums = [head_sum(jnp.concatenate([kk[j] * kk[j], r[j] * k2[j] * r_k[j]], axis=0)) for j in n]
        z = [jnp.dot(tri, jnp.concatenate(split(dlog[j]), axis=1), preferred_element_type=F32)
             for j in n]
        lg = [zz[:, :lanes] + zz[:, lanes:] for zz in z]
        kk = [kk[j] / jnp.maximum(jnp.sqrt(sums[j][:C]), 1e-12) for j in n]
        bv = [kk[j] * a[j] for j in n]
        e_neg = [jnp.exp(-l) for l in lg]
        e_end = [jnp.exp(l[C - 1:C, :] - l) for l in lg]
        left = [jnp.concatenate([-kk[j] * jnp.exp(lg[j] - dlog[j]), r[j] * jnp.exp(lg[j])],
                                axis=0).astype(BF16) for j in n]
        right = [jnp.concatenate([stack(bv[j] * e_neg[j]), stack(k2[j] * e_neg[j])], axis=0)
                 for j in n]
        aa = [bdot(left[j], right[j], NT_DIMS) for j in n]
        a_ab = [jnp.where(strict, m[:C, :lanes], 0.0) for m in aa]
        v2 = [stack(x) for x in v]
        t = [bdot(jnp.where(strict, aa[j][:C, lanes:], 0.0), v2[j]) for j in n]
        power = [bdot(m, stack(m)) for m in a_ab]
        inv = [eye + m for m in a_ab]
        for _ in range(2, int(math.log2(C))):
            z = [bdot(jnp.concatenate([inv[j], power[j]], axis=0), stack(power[j])) for j in n]
            inv = [inv[j] + z[j][:C] for j in n]
            power = [zz[C:] for zz in z]
        inv = [inv[j] + bdot(inv[j], stack(power[j])) for j in n]
        for j, (slot, rows, cols, _) in enumerate(prob):
            inv_s[slot] = inv[j].astype(BF16)
            t_s[slot] = t[j]
            arbk_s[slot] = jnp.where(jnp.concatenate([incl, incl], axis=1), aa[j][C:], 0.0).astype(BF16)
            left_s[slot] = left[j]
            dk_s[slot] = jnp.concatenate([bv[j] * e_end[j], k2[j] * e_end[j]], axis=0).astype(BF16)
            v_s[slot] = v[j].astype(BF16)
            v2_s[slot] = v2[j]
            bonus_s[slot] = sums[j][C:] * v[j]
            dec_s[slot] = jnp.exp(lg[j][C - 1:C, :])
        return 0

    def pass_b(c, _):
        n = range(n_pairs)
        slot = [c * n_pairs + p for p in n]
        state = [st_ref[p] for p in n]
        ls = [bdot(left_s[slot[p]], state[p], NT_DIMS) for p in n]
        u = [bdot(inv_s[slot[p]], stack(ls[p][:C] + t_s[slot[p]])) for p in n]
        u = [x.astype(BF16) for x in u]
        upd = [lax.dot_general(jnp.concatenate([u[p], v_s[slot[p]]], axis=0), dk_s[slot[p]], TN_DIMS,
                               preferred_element_type=F32) for p in n]
        for p in n:
            st_ref[p] = state[p] * dec_s[slot[p]] + jnp.where(same_head, upd[p], 0.0)
            u_s[slot[p]] = u[p]
            ls_s[slot[p]] = ls[p][C:]
        return 0

    def pass_c(i, _):
        prob = problems(i)
        n = range(len(prob))
        uv = [jnp.concatenate([stack(u_s[slot].astype(F32)), v2_s[slot]], axis=0)
              for slot, _, _, _ in prob]
        y = [ls_s[prob[j][0]] + jnp.dot(arbk_s[prob[j][0]], uv[j], preferred_element_type=F32)
             for j in n]
        mean = [head_sum(x) * np.float32(1.0 / nd) for x in y]
        d = [y[j] - mean[j] for j in n]
        var = [head_sum(x * x) * np.float32(1.0 / nd) for x in d]
        for j, (slot, rows, cols, _) in enumerate(prob):
            yn = d[j] * lax.rsqrt(var[j] + RWKV_GN_EPS) * lnw_ref[:, cols] + lnb_ref[:, cols]
            o_ref[rows, cols] = ((yn + bonus_s[slot]) * g_ref[rows, cols].astype(F32)).astype(o_ref.dtype)
        return 0

    lax.fori_loop(0, n_chunks // unroll, pass_a, 0)
    lax.fori_loop(0, n_chunks, pass_b, 0)
    lax.fori_loop(0, n_chunks // unroll, pass_c, 0)


def _wkv(r, wpre, k, v, a, g, v_first, v_gate, k_k, k_a, r_k, ln_w, ln_b, B, S):
    T, D = r.shape
    lanes = 2 * RWKV_HEAD_DIM
    has_vmix = v_first is not None
    n_pairs = next(n for n in (8, 4, 1) if D % (n * lanes) == 0)
    width = n_pairs * lanes
    ts = _pick_tile(S, 512, WKV_CHUNK)
    nt = S // ts
    seq = pl.BlockSpec((ts, width), lambda b, h, t: (b * nt + t, h))
    par = pl.BlockSpec((1, width), lambda b, h, t: (0, h))
    args = [r, wpre, k, v, a, g] + ([v_first, v_gate] if has_vmix else [])
    params = [p.reshape(1, D) for p in (k_k, k_a, r_k, ln_w, ln_b)]
    C = WKV_CHUNK
    n_chunks = ts // C
    slots = n_chunks * n_pairs
    scratch = [pltpu.VMEM((n_pairs, lanes, lanes), F32),
               pltpu.VMEM((slots, C, lanes), BF16),
               pltpu.VMEM((slots, C, lanes), F32),
               pltpu.VMEM((slots, C, 2 * lanes), BF16),
               pltpu.VMEM((slots, 2 * C, lanes), BF16),
               pltpu.VMEM((slots, 2 * C, lanes), BF16),
               pltpu.VMEM((slots, C, lanes), BF16),
               pltpu.VMEM((slots, 2 * C, lanes), BF16),
               pltpu.VMEM((slots, C, lanes), F32),
               pltpu.VMEM((slots, 1, lanes), F32),
               pltpu.VMEM((slots, C, lanes), BF16),
               pltpu.VMEM((slots, C, lanes), F32)]
    scratch_bytes = slots * C * lanes * 30
    return pl.pallas_call(
        functools.partial(_wkv_body, n_chunks=n_chunks, n_pairs=n_pairs, has_vmix=has_vmix),
        grid=(B, D // width, nt),
        in_specs=[seq] * len(args) + [par] * len(params),
        out_specs=seq,
        out_shape=jax.ShapeDtypeStruct((T, D), BF16),
        scratch_shapes=scratch,
        compiler_params=_cparams(3, 2 * ts * width * 4 * (len(args) + 1) + scratch_bytes
                                 + 16 * 1024 * 1024),
        name="wkv7_chunked",
    )(*args, *params)


def _pad_to(x, n, axis):
    pad = n - x.shape[axis]
    if pad == 0:
        return x
    widths = [(0, 0)] * x.ndim
    widths[axis] = (0, pad)
    return jnp.pad(x, widths)


def _lora(x, w_down, w_up, *, bias=None, act_mid=None, act_out=None, out_dtype=F32, name="lora"):
    inner = -(-w_down.shape[1] // LANE) * LANE
    wd = _pad_to(w_down, inner, 1).astype(BF16)
    wu = _pad_to(w_up, inner, 0).astype(BF16)
    mid = _matmul([(x, wd)], out_dtype=BF16, act=act_mid, name=name + "_down")
    return _matmul([(mid, wu)], out_dtype=out_dtype, bias=bias, act=act_out, name=name + "_up")


def _stacked(w):
    if isinstance(w, tuple):
        return w
    return w.astype(BF16)[None], 0


def _hybrid_layer(x, B, S, norm, w_in, q_norm, k_norm, lam_re, lam_im, log_dt, b_re, b_im, c_re, c_im,
                  d_skip, glu_w, glu_b, w_out):
    (w_in, l_in), (glu_w, l_glu), (w_out, l_out) = _stacked(w_in), _stacked(glu_w), _stacked(w_out)
    sw = d_skip.shape[0]
    aw = w_out.shape[1] - sw
    assert aw == sw
    n_heads = aw // ATTN_HEAD_DIM
    h = _rmsnorm(x, norm)
    qkvu = _matmul([(h, w_in, l_in, 0)], out_dtype=BF16, name="hybrid_in_proj")
    y_attn = _attention(qkvu, q_norm, k_norm, B, S, n_heads)
    ops = _s5_chunk_operators(lam_re, lam_im, log_dt, b_re, b_im, c_re, c_im)
    z = _s5(qkvu[:, 3 * aw:], ops, d_skip, B, S)
    y_ssm = _matmul([(z, glu_w, l_glu, 0)], out_dtype=BF16, bias=glu_b, act="sigmoid", mul=z,
                    name="ssm_glu")
    return _matmul([(y_attn, w_out, l_out, 0), (y_ssm, w_out, l_out, 1)], out_dtype=F32, residual=x,
                   name="hybrid_out_proj")


def _rwkv_layer(x, B, S, norm, mu, w_r, w_k, w_v, w0, w1, w2, a0, a1, a2, g1, g2, k_k, k_a, r_k,
                ln_w, ln_b, w_o, v_first, v_mix):
    l_r, l_w, l_k, l_v, l_a, l_g = _rwkv_lerp(x, norm, mu, S)
    r = _matmul([(l_r, *_stacked(w_r), 0)], out_dtype=F32, name="rwkv_r")
    k = _matmul([(l_k, *_stacked(w_k), 0)], out_dtype=F32, name="rwkv_k")
    v = _matmul([(l_v, *_stacked(w_v), 0)], out_dtype=F32, name="rwkv_v")
    wpre = _lora(l_w, w1, w2, bias=w0, act_mid="tanh", name="rwkv_decay")
    a = _lora(l_a, a1, a2, bias=a0, act_out="sigmoid", name="rwkv_iclr")
    g = _lora(l_g, g1, g2, act_mid="sigmoid", out_dtype=BF16, name="rwkv_gate")
    if v_mix is None:
        v_gate, vf = None, None
        v_first = v
    else:
        v0, v1, v2 = v_mix
        v_gate = _lora(l_v, v1, v2, bias=v0, act_out="sigmoid", name="rwkv_vmix")
        vf = v_first
    y = _wkv(r, wpre, k, v, a, g, vf, v_gate, k_k, k_a, r_k.reshape(-1), ln_w, ln_b, B, S)
    return _matmul([(y, *_stacked(w_o), 0)], out_dtype=F32, residual=x, name="rwkv_out_proj"), v_first


def _ffn_layer(x, B, S, norm, w_up, conv_w, conv_b, w_down):
    h = _rmsnorm(x, norm)
    act = _ffn_up_act(h, *_stacked(w_up), conv_w, conv_b, S)
    return _matmul([(act, *_stacked(w_down), 0)], out_dtype=F32, residual=x, name="ffn_down")


def kernel(x, mix_norm, hy_w_in, attn_q_norm, attn_k_norm, ssm_lambda_re, ssm_lambda_im, ssm_log_dt, ssm_b_re, ssm_b_im, ssm_c_re, ssm_c_im, ssm_d, ssm_glu_w, ssm_glu_b, hy_w_out, rwkv_mu, rwkv_w_r, rwkv_w_k, rwkv_w_v, rwkv_w0, rwkv_w1, rwkv_w2, rwkv_a0, rwkv_a1, rwkv_a2, rwkv_g1, rwkv_g2, rwkv_k_k, rwkv_k_a, rwkv_r_k, rwkv_ln_w, rwkv_ln_b, rwkv_w_o, rwkv_v0, rwkv_v1, rwkv_v2, ffn_norm, ffn_w_up, ffn_conv_w, ffn_conv_b, ffn_w_down):
    B, S, D = x.shape
    depth = mix_norm.shape[0]
    x = x.reshape(B * S, D)
    hy_w_in, ssm_glu_w, hy_w_out, rwkv_w_r, rwkv_w_k, rwkv_w_v, rwkv_w_o, ffn_w_up, ffn_w_down = (
        w.astype(BF16) for w in (hy_w_in, ssm_glu_w, hy_w_out, rwkv_w_r, rwkv_w_k, rwkv_w_v,
                                 rwkv_w_o, ffn_w_up, ffn_w_down))
    v_first = None
    for layer in range(depth):
        i = layer // 2
        if layer % 2 == 0:
            x = _hybrid_layer(x, B, S, mix_norm[layer], (hy_w_in, i), attn_q_norm[i], attn_k_norm[i],
                              ssm_lambda_re[i], ssm_lambda_im[i], ssm_log_dt[i], ssm_b_re[i],
                              ssm_b_im[i], ssm_c_re[i], ssm_c_im[i], ssm_d[i], (ssm_glu_w, i),
                              ssm_glu_b[i], (hy_w_out, i))
        else:
            v_mix = None if i == 0 else (rwkv_v0[i - 1], rwkv_v1[i - 1], rwkv_v2[i - 1])
            x, v_first = _rwkv_layer(x, B, S, mix_norm[layer], rwkv_mu[i], (rwkv_w_r, i), (rwkv_w_k, i),
                                     (rwkv_w_v, i), rwkv_w0[i], rwkv_w1[i], rwkv_w2[i], rwkv_a0[i],
                                     rwkv_a1[i], rwkv_a2[i], rwkv_g1[i], rwkv_g2[i], rwkv_k_k[i],
                                     rwkv_k_a[i], rwkv_r_k[i], rwkv_ln_w[i], rwkv_ln_b[i],
                                     (rwkv_w_o, i), v_first, v_mix)
        x = _ffn_layer(x, B, S, ffn_norm[layer], (ffn_w_up, layer), ffn_conv_w[layer],
                       ffn_conv_b[layer], (ffn_w_down, layer))
    return x.reshape(B, S, D)
```

```python
import functools
import math

import numpy as np
import jax
import jax.numpy as jnp
from jax import lax
from jax.experimental import pallas as pl
from jax.experimental.pallas import tpu as pltpu

F32 = jnp.float32
BF16 = jnp.bfloat16
HIGHEST = lax.Precision.HIGHEST

RMS_EPS = 1e-6
ATTN_HEAD_DIM = 128
DILATED_BRANCHES = ((128, 1), (512, 4), (2048, 16))
ROPE_THETA = 10000.0
SSM_GROUP = 16
SSM_STATE = 64
SSM_CHUNK = 16
RWKV_HEAD_DIM = 64
RWKV_GN_EPS = 64e-5
WKV_CHUNK = 64
FFN_CONV = 3

V7X_VMEM_BYTES = 64 * 1024 * 1024
VMEM_BUDGET = 48 * 1024 * 1024
LANE = 128
SUBLANE = 8

NT_DIMS = (((1,), (1,)), ((), ()))
TN_DIMS = (((0,), (0,)), ((), ()))


def _cparams(n_grid, vmem_bytes):
    limit = int(min(max(vmem_bytes, 16 * 1024 * 1024), V7X_VMEM_BYTES - 6 * 1024 * 1024))
    return pltpu.CompilerParams(dimension_semantics=("arbitrary",) * n_grid,
                                vmem_limit_bytes=limit)


def _pick_tile(n, pref, align):
    t = min(pref, n)
    t -= t % align
    while t >= align:
        if n % t == 0:
            return t
        t -= align
    return n


def _sigmoid(x):
    return 1.0 / (1.0 + jnp.exp(-x))


def _gelu(x):
    return 0.5 * x * (1.0 + lax.erf(x * np.float32(math.sqrt(0.5))))


def _rms(x, gain):
    return x * lax.rsqrt(jnp.mean(x * x, axis=-1, keepdims=True) + RMS_EPS) * gain


def _rmsnorm_body(x_ref, g_ref, o_ref):
    o_ref[...] = _rms(x_ref[...], g_ref[...]).astype(o_ref.dtype)


def _rmsnorm(x, gain, out_dtype=BF16):
    T, D = x.shape
    tm = _pick_tile(T, 256, SUBLANE)
    return pl.pallas_call(
        _rmsnorm_body,
        grid=(T // tm,),
        in_specs=[pl.BlockSpec((tm, D), lambda i: (i, 0)),
                  pl.BlockSpec((1, D), lambda i: (0, 0))],
        out_specs=pl.BlockSpec((tm, D), lambda i: (i, 0)),
        out_shape=jax.ShapeDtypeStruct((T, D), out_dtype),
        compiler_params=_cparams(1, 6 * tm * D * 4),
        name="rmsnorm",
    )(x, gain.reshape(1, D))


def _lerp_body(x_ref, xh_ref, g_ref, mu_ref, *o_refs, tm, seq):
    i = pl.program_id(0)
    g = g_ref[...]
    h = _rms(x_ref[...], g)
    hh = _rms(xh_ref[...], g)
    prev = jnp.where((i * tm) % seq == 0, 0.0, hh[SUBLANE - 1:SUBLANE, :])
    hs = pltpu.roll(h, 1, axis=0)
    row = lax.broadcasted_iota(jnp.int32, h.shape, 0)
    xx = jnp.where(row == 0, prev, hs) - h
    for n, o_ref in enumerate(o_refs):
        o_ref[...] = (h + xx * mu_ref[n:n + 1, :]).astype(o_ref.dtype)


def _rwkv_lerp(x, gain, mu, seq):
    T, D = x.shape
    n = mu.shape[0]
    tm = _pick_tile(seq, 128, SUBLANE)
    hb = tm // SUBLANE
    return pl.pallas_call(
        functools.partial(_lerp_body, tm=tm, seq=seq),
        grid=(T // tm,),
        in_specs=[pl.BlockSpec((tm, D), lambda i: (i, 0)),
                  pl.BlockSpec((SUBLANE, D), lambda i: (jnp.maximum(i * hb - 1, 0), 0)),
                  pl.BlockSpec((1, D), lambda i: (0, 0)),
                  pl.BlockSpec((n, D), lambda i: (0, 0))],
        out_specs=[pl.BlockSpec((tm, D), lambda i: (i, 0))] * n,
        out_shape=[jax.ShapeDtypeStruct((T, D), BF16)] * n,
        compiler_params=_cparams(1, tm * D * (2 * 4 + 2 * 2 * n + 4 * 4)),
        name="rwkv_lerp",
    )(x, x, gain.reshape(1, D), mu)


def _mm_body(*refs, n_pairs, has_bias, act, has_mul, has_res):
    acc = None
    idx = 0
    for _ in range(n_pairs):
        d = jnp.dot(refs[idx][...], refs[idx + 1][...], preferred_element_type=F32)
        acc = d if acc is None else acc + d
        idx += 2
    if has_bias:
        acc = acc + refs[idx][...]
        idx += 1
    if act == "tanh":
        acc = jnp.tanh(acc)
    elif act == "sigmoid":
        acc = _sigmoid(acc)
    if has_mul:
        acc = acc * refs[idx][...].astype(F32)
        idx += 1
    if has_res:
        acc = acc + refs[idx][...]
        idx += 1
    o_ref = refs[idx]
    o_ref[...] = acc.astype(o_ref.dtype)


def _mm_tiles(M, N, k_total, out_bytes, extra_bytes):
    for tm_pref, tn_pref in ((1024, 512), (512, 512), (512, 256), (256, 256), (128, 256), (128, 128)):
        tm = _pick_tile(M, tm_pref, 16)
        tn = _pick_tile(N, tn_pref, LANE)
        need = (2 * tm * k_total * 2 + 2 * k_total * tn * 2
                + tm * tn * (2 * out_bytes + 2 * extra_bytes + 8))
        if need <= VMEM_BUDGET:
            return tm, tn, need
    return tm, tn, need


def _matmul(pairs, *, out_dtype, bias=None, act=None, mul=None, residual=None, name="matmul"):
    M = pairs[0][0].shape[0]
    N = pairs[0][1].shape[-1]
    k_total = sum(p[0].shape[1] for p in pairs)
    out_bytes = jnp.dtype(out_dtype).itemsize
    extra = (2 if mul is not None else 0) + (4 if residual is not None else 0)
    tm, tn, need = _mm_tiles(M, N, k_total, out_bytes, extra)
    args, in_specs = [], []
    for pair in pairs:
        a, w = pair[0], pair[1]
        layer, kblock = (pair[2], pair[3]) if len(pair) == 4 else (0, 0)
        if w.ndim == 2:
            w = w[None]
        K = a.shape[1]
        assert a.dtype == BF16 and w.dtype == BF16 and a.shape[0] == M
        assert w.shape[2] == N and w.shape[1] % K == 0
        args += [a, w]
        in_specs += [pl.BlockSpec((tm, K), lambda i, j: (i, 0)),
                     pl.BlockSpec((None, K, tn), lambda i, j, layer=layer, kblock=kblock: (layer, kblock, j))]
    if bias is not None:
        args.append(bias.reshape(1, N).astype(F32))
        in_specs.append(pl.BlockSpec((1, tn), lambda i, j: (0, j)))
    if mul is not None:
        args.append(mul)
        in_specs.append(pl.BlockSpec((tm, tn), lambda i, j: (i, j)))
    if residual is not None:
        args.append(residual)
        in_specs.append(pl.BlockSpec((tm, tn), lambda i, j: (i, j)))
    return pl.pallas_call(
        functools.partial(_mm_body, n_pairs=len(pairs), has_bias=bias is not None, act=act,
                          has_mul=mul is not None, has_res=residual is not None),
        grid=(M // tm, N // tn),
        in_specs=in_specs,
        out_specs=pl.BlockSpec((tm, tn), lambda i, j: (i, j)),
        out_shape=jax.ShapeDtypeStruct((M, N), out_dtype),
        compiler_params=_cparams(2, need + 8 * 1024 * 1024),
        name=name,
    )(*args)


def _attn_bias(S, tq):
    delta = (S - tq) + np.arange(tq)[:, None] - np.arange(S)[None, :]
    mult = np.zeros(delta.shape, np.float64)
    for window, dilation in DILATED_BRANCHES:
        mult += (delta >= 0) & (delta <= window) & (delta % dilation == 0)
    bias = np.where(mult > 0, np.log(np.maximum(mult, 1.0)), -1e30)
    return jnp.asarray(bias, F32)


def _rope_tables(S):
    half = ATTN_HEAD_DIM // 2
    inv_freq = ROPE_THETA ** (-jnp.arange(half, dtype=F32) / half)
    ang = jnp.arange(S, dtype=jnp.int32).astype(F32)[:, None] * inv_freq[None, :]
    cos, sin = jnp.cos(ang), jnp.sin(ang)
    return jnp.concatenate([cos, cos], axis=-1), jnp.concatenate([-sin, sin], axis=-1)


def _attn_body(q_ref, k_ref, v_ref, qg_ref, kg_ref, cos_ref, sin_ref, bias_ref, o_ref,
               qs_ref, ks_ref, *, S, tq):
    half = ATTN_HEAD_DIM // 2

    def prep(x_ref, g_ref, scale):
        x = _rms(x_ref[...].astype(F32), g_ref[...])
        x = x * cos_ref[...] + pltpu.roll(x, half, axis=1) * sin_ref[...]
        return (x * scale).astype(BF16)

    qs_ref[...] = prep(q_ref, qg_ref, np.float32(ATTN_HEAD_DIM ** -0.5))
    ks_ref[...] = prep(k_ref, kg_ref, np.float32(1.0))

    n_blocks = S // tq

    def scores(i):
        return lax.dot_general(qs_ref[i * tq:(i + 1) * tq, :], ks_ref[:(i + 1) * tq, :], NT_DIMS,
                               preferred_element_type=F32)

    s_next = scores(0)
    for i in range(n_blocks):
        s = s_next + bias_ref[:, (n_blocks - 1 - i) * tq:]
        if i + 1 < n_blocks:
            s_next = scores(i + 1)
        p = jnp.exp(s - jnp.max(s, axis=-1, keepdims=True))
        l = jnp.sum(p, axis=-1, keepdims=True)
        acc = jnp.dot(p.astype(BF16), v_ref[:(i + 1) * tq, :], preferred_element_type=F32)
        o_ref[i * tq:(i + 1) * tq, :] = (acc / l).astype(o_ref.dtype)


def _attention(qkvu, q_gain, k_gain, B, S, n_heads):
    T = B * S
    dh = ATTN_HEAD_DIM
    tq = _pick_tile(S, 256, LANE)
    cos, sin = _rope_tables(S)
    bias = _attn_bias(S, tq)
    head = lambda off: pl.BlockSpec((S, dh), lambda b, h: (b, off + h))
    const2 = lambda shape: pl.BlockSpec(shape, lambda b, h: (0, 0))
    return pl.pallas_call(
        functools.partial(_attn_body, S=S, tq=tq),
        grid=(B, n_heads),
        in_specs=[head(0), head(n_heads), head(2 * n_heads),
                  const2((1, dh)), const2((1, dh)), const2((S, dh)), const2((S, dh)),
                  const2(bias.shape)],
        out_specs=pl.BlockSpec((S, dh), lambda b, h: (b, h)),
        out_shape=jax.ShapeDtypeStruct((T, n_heads * dh), BF16),
        scratch_shapes=[pltpu.VMEM((S, dh), BF16), pltpu.VMEM((S, dh), BF16)],
        compiler_params=_cparams(2, 2 * bias.size * 4 + 24 * S * dh * 4),
        name="dilated_attention",
    )(qkvu, qkvu, qkvu, q_gain.reshape(1, dh), k_gain.reshape(1, dh), cos, sin, bias)


def _s5_chunk_operators(lam_re, lam_im, log_dt, b_re, b_im, c_re, c_im):
    Lc = SSM_CHUNK
    G, N = lam_re.shape
    P = b_re.shape[-1]
    dt = jnp.exp(log_dt)[:, None]
    mag = jnp.exp(lam_re * dt)
    ab_re, ab_im = mag * jnp.cos(lam_im * dt), mag * jnp.sin(lam_im * dt)
    den = lam_re * lam_re + lam_im * lam_im
    nr, ni = ab_re - 1.0, ab_im
    coef_re = (nr * lam_re + ni * lam_im) / den
    coef_im = (ni * lam_re - nr * lam_im) / den
    bb_re = coef_re[..., None] * b_re - coef_im[..., None] * b_im
    bb_im = coef_re[..., None] * b_im + coef_im[..., None] * b_re
    steps = jnp.arange(Lc + 1, dtype=F32)[:, None, None]
    pmag = jnp.exp(lam_re * dt * steps)
    pw_re, pw_im = pmag * jnp.cos(lam_im * dt * steps), pmag * jnp.sin(lam_im * dt * steps)
    ca_re = c_re[None] * pw_re[:Lc, :, None, :] - c_im[None] * pw_im[:Lc, :, None, :]
    ca_im = c_re[None] * pw_im[:Lc, :, None, :] + c_im[None] * pw_re[:Lc, :, None, :]
    lag = (jnp.sum(ca_re[..., None] * bb_re[None, :, None], axis=3)
           - jnp.sum(ca_im[..., None] * bb_im[None, :, None], axis=3))
    ti = jnp.arange(Lc)[:, None]
    to = jnp.arange(Lc)[None, :]
    sel = jnp.clip(to - ti, 0, Lc - 1)
    toep = jnp.where((to >= ti)[:, :, None, None, None], lag[sel], 0.0)
    toep = toep.transpose(2, 0, 4, 1, 3).reshape(G, Lc * P, Lc * P)
    rev = pw_re[Lc - 1 - jnp.arange(Lc)], pw_im[Lc - 1 - jnp.arange(Lc)]
    bin_re = rev[0][..., None] * bb_re[None] - rev[1][..., None] * bb_im[None]
    bin_im = rev[0][..., None] * bb_im[None] + rev[1][..., None] * bb_re[None]
    bin_re = bin_re.transpose(1, 0, 3, 2).reshape(G, Lc * P, N)
    bin_im = bin_im.transpose(1, 0, 3, 2).reshape(G, Lc * P, N)
    nxt_re, nxt_im = pw_re[1:], pw_im[1:]
    co_re = c_re[None] * nxt_re[:, :, None, :] - c_im[None] * nxt_im[:, :, None, :]
    co_im = c_re[None] * nxt_im[:, :, None, :] + c_im[None] * nxt_re[:, :, None, :]
    cout_re = co_re.transpose(1, 3, 0, 2).reshape(G, N, Lc * P)
    cout_im = (-co_im).transpose(1, 3, 0, 2).reshape(G, N, Lc * P)
    return (toep.astype(BF16), bin_re.astype(BF16), bin_im.astype(BF16),
            cout_re.astype(BF16), cout_im.astype(BF16),
            pw_re[Lc][:, None, :], pw_im[Lc][:, None, :])


def _s5_lane_blocks(ops):
    toep, bin_re, bin_im, cout_re, cout_im, a_re, a_im = ops
    G, W, N = bin_re.shape
    Lc, P = SSM_CHUNK, SSM_GROUP
    gb = LANE // P
    nb = G // gb
    eye = jnp.eye(gb, dtype=BF16)
    t6 = toep.reshape(nb, gb, Lc, P, Lc, P)
    t8 = t6[:, :, :, :, :, None, :] * eye[None, :, None, None, None, :, None]
    t8 = t8.transpose(0, 2, 1, 3, 4, 5, 6).reshape(nb, Lc * LANE, Lc * LANE)

    def b8(x):
        x = x.reshape(nb, gb, Lc, P, N)[:, :, :, :, None, :] * eye[None, :, None, None, :, None]
        return x.transpose(0, 2, 1, 3, 4, 5).reshape(nb, Lc * LANE, gb * N)

    def c8(x):
        x = x.reshape(nb, gb, N, Lc, P)[:, :, :, :, None, :] * eye[None, :, None, None, :, None]
        return x.reshape(nb, gb * N, Lc * LANE)

    return (t8, b8(bin_re), b8(bin_im), c8(cout_re), c8(cout_im),
            a_re.reshape(nb, 1, gb * N), a_im.reshape(nb, 1, gb * N))


def _s5_body(u_ref, toep_ref, bre_ref, bim_ref, cre_ref, cim_ref, are_ref, aim_ref, d_ref, z_ref,
             xin_re, xin_im, xst_re, xst_im, car_re, car_im, *, n_chunks, batch):
    @pl.when(pl.program_id(1) == 0)
    def _():
        car_re[...] = jnp.zeros_like(car_re)
        car_im[...] = jnp.zeros_like(car_im)

    u = u_ref[0]
    xin_re[...] = jnp.dot(u, bre_ref[0], preferred_element_type=F32)
    xin_im[...] = jnp.dot(u, bim_ref[0], preferred_element_type=F32)
    a_re, a_im = are_ref[0], aim_ref[0]

    def step(c, carry):
        x_re, x_im = carry
        rows = pl.ds(pl.multiple_of(c * batch, batch), batch)
        xst_re[rows, :] = x_re
        xst_im[rows, :] = x_im
        return (a_re * x_re - a_im * x_im + xin_re[rows, :],
                a_re * x_im + a_im * x_re + xin_im[rows, :])

    x_re, x_im = lax.fori_loop(0, n_chunks, step, (car_re[...], car_im[...]))
    car_re[...] = x_re
    car_im[...] = x_im
    y = (jnp.dot(u, toep_ref[0], preferred_element_type=F32)
         + jnp.dot(xst_re[...].astype(BF16), cre_ref[0], preferred_element_type=F32)
         + jnp.dot(xst_im[...].astype(BF16), cim_ref[0], preferred_element_type=F32))
    y = y + d_ref[0] * u.astype(F32)
    z_ref[0] = _gelu(y).astype(z_ref.dtype)


def _s5(u, ops, d_skip, B, S):
    toep, bin_re, bin_im, cout_re, cout_im, a_re, a_im = _s5_lane_blocks(ops)
    nb, W, N = bin_re.shape
    Lc = SSM_CHUNK
    nc = S // Lc
    parts = 4 if nc % 4 == 0 else 1
    rows = (nc // parts) * B
    ut = u.reshape(B, nc, Lc, nb, LANE).transpose(3, 1, 0, 2, 4).reshape(nb, nc * B, W)
    d_t = jnp.tile(d_skip.reshape(nb, 1, LANE), (1, Lc, 1)).reshape(nb, 1, W)
    const = lambda shape: pl.BlockSpec((1,) + shape, lambda l, r: (l, 0, 0))
    block = pl.BlockSpec((1, rows, W), lambda l, r: (l, r, 0))
    zt = pl.pallas_call(
        functools.partial(_s5_body, n_chunks=nc // parts, batch=B),
        grid=(nb, parts),
        in_specs=[block, const((W, W)), const((W, N)), const((W, N)), const((N, W)), const((N, W)),
                  const((1, N)), const((1, N)), const((1, W))],
        out_specs=block,
        out_shape=jax.ShapeDtypeStruct((nb, nc * B, W), BF16),
        scratch_shapes=[pltpu.VMEM((rows, N), F32)] * 4 + [pltpu.VMEM((B, N), F32)] * 2,
        compiler_params=_cparams(2, 2 * (W * W + 4 * W * N) * 2 + rows * W * 24 + rows * N * 32),
        name="s5_chunked_scan",
    )(ut, toep, bin_re, bin_im, cout_re, cout_im, a_re, a_im, d_t)
    return zt.reshape(nb, nc, B, Lc, LANE).transpose(2, 1, 3, 0, 4).reshape(B * S, nb * LANE)


def _ffn_up_body(h_ref, wg_ref, wv_ref, cw_ref, cb_ref, o_ref, gate_ref, val_ref, tail_ref,
                 *, tm, seq, nj):
    i, j = pl.program_id(0), pl.program_id(1)

    def conv(g, shifted):
        y = cw_ref[FFN_CONV - 1:FFN_CONV, :] * g + cb_ref[...]
        for back in range(1, FFN_CONV):
            y = y + cw_ref[FFN_CONV - 1 - back:FFN_CONV - back, :] * shifted(back)
        return y

    def activate():
        jj = j - 1
        gate, val = gate_ref[...], val_ref[...]
        y = conv(gate, lambda back: pltpu.roll(gate, back, axis=0))
        o_ref[...] = (_gelu(y) * val).astype(o_ref.dtype)
        prev = jnp.where((i * tm) % seq == 0, 0.0, tail_ref[jj])
        top = gate[:SUBLANE, :]
        row = lax.broadcasted_iota(jnp.int32, top.shape, 0)
        y_top = conv(top, lambda back: jnp.where(row < back, pltpu.roll(prev, back, axis=0),
                                                 pltpu.roll(top, back, axis=0)))
        o_ref[:SUBLANE, :] = (_gelu(y_top) * val[:SUBLANE, :]).astype(o_ref.dtype)
        tail_ref[jj] = gate[tm - SUBLANE:, :]

    def multiply():
        h = h_ref[...]
        gate_ref[...] = jnp.dot(h, wg_ref[...], preferred_element_type=F32)
        val_ref[...] = jnp.dot(h, wv_ref[...], preferred_element_type=F32)

    @pl.when(jnp.logical_and(i == 0, j == 0))
    def _():
        tail_ref[...] = jnp.zeros_like(tail_ref)

    @pl.when(j == 0)
    def _():
        multiply()

    @pl.when(jnp.logical_and(j > 0, j < nj))
    def _():
        activate()
        multiply()

    @pl.when(j == nj)
    def _():
        activate()


def _ffn_up_act(h, w_up, layer, conv_w, conv_b, seq):
    T, D = h.shape
    Fd = w_up.shape[2] // 2
    tm = _pick_tile(seq, 1024, 16)
    tn = _pick_tile(Fd, 512, LANE)
    nj = Fd // tn
    cur = lambda j: jnp.minimum(j, nj - 1)
    lag = lambda j: jnp.maximum(j - 1, 0)
    need = 2 * tm * D * 2 + 4 * D * tn * 2 + 2 * tm * tn * 2 + 12 * tm * tn * 4
    return pl.pallas_call(
        functools.partial(_ffn_up_body, tm=tm, seq=seq, nj=nj),
        grid=(T // tm, nj + 1),
        in_specs=[pl.BlockSpec((tm, D), lambda i, j: (i, 0)),
                  pl.BlockSpec((None, D, tn), lambda i, j: (layer, 0, cur(j))),
                  pl.BlockSpec((None, D, tn), lambda i, j: (layer, 0, nj + cur(j))),
                  pl.BlockSpec((FFN_CONV, tn), lambda i, j: (0, lag(j))),
                  pl.BlockSpec((1, tn), lambda i, j: (0, lag(j)))],
        out_specs=pl.BlockSpec((tm, tn), lambda i, j: (i, lag(j))),
        out_shape=jax.ShapeDtypeStruct((T, Fd), BF16),
        scratch_shapes=[pltpu.VMEM((tm, tn), F32), pltpu.VMEM((tm, tn), F32),
                        pltpu.VMEM((nj, SUBLANE, tn), F32)],
        compiler_params=_cparams(2, need),
        name="ffn_up_convglu",
    )(h, w_up, w_up, conv_w, conv_b.reshape(1, Fd))


def _wkv_body(*refs, n_chunks, n_pairs, has_vmix):
    C = WKV_CHUNK
    nd = RWKV_HEAD_DIM
    lanes = 2 * nd
    n_seq = 8 if has_vmix else 6
    r_ref, w_ref, k_ref, v_ref, a_ref, g_ref = refs[:6]
    vf_ref, vg_ref = refs[6:8] if has_vmix else (None, None)
    kk_ref, ka_ref, rk_ref, lnw_ref, lnb_ref, o_ref = refs[n_seq:n_seq + 6]
    (st_ref, inv_s, t_s, arbk_s, left_s, dk_s, v_s, v2_s, bonus_s, dec_s, u_s, ls_s) = refs[n_seq + 6:]
    unroll = 2 if (n_chunks % 2 == 0 and n_pairs <= 4) else 1

    @pl.when(pl.program_id(2) == 0)
    def _():
        st_ref[...] = jnp.zeros_like(st_ref)

    lane_r = lax.broadcasted_iota(jnp.int32, (lanes, lanes), 0)
    lane_c = lax.broadcasted_iota(jnp.int32, (lanes, lanes), 1)
    same_head = (lane_r // nd) == (lane_c // nd)
    head_ones = same_head.astype(BF16)
    pos_r = lax.broadcasted_iota(jnp.int32, (C, lanes), 0)
    pos_c = lax.broadcasted_iota(jnp.int32, (C, lanes), 1) % C
    strict = pos_r > pos_c
    incl = pos_r >= pos_c
    eye = (pos_r == pos_c).astype(F32)
    tri = (lax.broadcasted_iota(jnp.int32, (C, C), 0)
           >= lax.broadcasted_iota(jnp.int32, (C, C), 1)).astype(BF16)
    lane1 = lax.broadcasted_iota(jnp.int32, (1, lanes), 1)
    m0 = (lane1 < nd).astype(F32)
    m1 = 1.0 - m0

    def split(x):
        hi = x.astype(BF16)
        return hi, (x - hi.astype(F32)).astype(BF16)

    def head_sum(x):
        n = x.shape[0]
        z = jnp.dot(jnp.concatenate(split(x), axis=0), head_ones, preferred_element_type=F32)
        return z[:n] + z[n:]

    def stack(x):
        return jnp.concatenate([x * m0, x * m1], axis=0).astype(BF16)

    def bdot(a, b, dims=None):
        a, b = a.astype(BF16), b.astype(BF16)
        if dims is None:
            return jnp.dot(a, b, preferred_element_type=F32)
        return lax.dot_general(a, b, dims, preferred_element_type=F32)

    def problems(i):
        out = []
        for cc in range(unroll):
            c = i * unroll + cc
            for p in range(n_pairs):
                out.append((c * n_pairs + p, pl.ds(pl.multiple_of(c * C, C), C),
                            slice(p * lanes, (p + 1) * lanes), p))
        return out

    def pass_a(i, _):
        prob = problems(i)
        ld = lambda ref: [ref[rows, cols] for _, rows, cols, _ in prob]
        par = lambda ref: [ref[:, cols] for _, _, cols, _ in prob]
        n = range(len(prob))
        r, k, v, a, wpre = ld(r_ref), ld(k_ref), ld(v_ref), ld(a_ref), ld(w_ref)
        k_k, k_a, r_k = par(kk_ref), par(ka_ref), par(rk_ref)
        if has_vmix:
            vf, vg = ld(vf_ref), ld(vg_ref)
            v = [v[j] + (vf[j] - v[j]) * vg[j] for j in n]
        dlog = [-jnp.exp(-(jnp.maximum(-w, 0.0) + jnp.log(1.0 + jnp.exp(-jnp.abs(w)))) - 0.5)
                for w in wpre]
        kk = [k[j] * k_k[j] for j in n]
        k2 = [k[j] * (1.0 + (a[j] - 1.0) * k_a[j]) for j in n]
        sums = [head_sum(jnp.concatenate([kk[j] * kk[j], r[j] * k2[j] * r_k[j]], axis=0)) for j in n]
        z = [jnp.dot(tri, jnp.concatenate(split(dlog[j]), axis=1), preferred_element_type=F32)
             for j in n]
        lg = [zz[:, :lanes] + zz[:, lanes:] for zz in z]
        kk = [kk[j] / jnp.maximum(jnp.sqrt(sums[j][:C]), 1e-12) for j in n]
        bv = [kk[j] * a[j] for j in n]
        e_neg = [jnp.exp(-l) for l in lg]
        e_end = [jnp.exp(l[C - 1:C, :] - l) for l in lg]
        left = [jnp.concatenate([-kk[j] * jnp.exp(lg[j] - dlog[j]), r[j] * jnp.exp(lg[j])],
                                axis=0).astype(BF16) for j in n]
        right = [jnp.concatenate([stack(bv[j] * e_neg[j]), stack(k2[j] * e_neg[j])], axis=0)
                 for j in n]
        aa = [bdot(left[j], right[j], NT_DIMS) for j in n]
        a_ab = [jnp.where(strict, m[:C, :lanes], 0.0) for m in aa]
        v2 = [stack(x) for x in v]
        t = [bdot(jnp.where(strict, aa[j][:C, lanes:], 0.0), v2[j]) for j in n]
        power = [bdot(m, stack(m)) for m in a_ab]
        inv = [eye + m for m in a_ab]
        for _ in range(2, int(math.log2(C))):
            z = [bdot(jnp.concatenate([inv[j], power[j]], axis=0), stack(power[j])) for j in n]
            inv = [inv[j] + z[j][:C] for j in n]
            power = [zz[C:] for zz in z]
        inv = [inv[j] + bdot(inv[j], stack(power[j])) for j in n]
        for j, (slot, rows, cols, _) in enumerate(prob):
            inv_s[slot] = inv[j].astype(BF16)
            t_s[slot] = t[j]
            arbk_s[slot] = jnp.where(jnp.concatenate([incl, incl], axis=1), aa[j][C:], 0.0).astype(BF16)
            left_s[slot] = left[j]
            dk_s[slot] = jnp.concatenate([bv[j] * e_end[j], k2[j] * e_end[j]], axis=0).astype(BF16)
            v_s[slot] = v[j].astype(BF16)
            v2_s[slot] = v2[j]
            bonus_s[slot] = sums[j][C:] * v[j]
            dec_s[slot] = jnp.exp(lg[j][C - 1:C, :])
        return 0

    def pass_b(c, _):
        n = range(n_pairs)
        slot = [c * n_pairs + p for p in n]
        state = [st_ref[p] for p in n]
        ls = [bdot(left_s[slot[p]], state[p], NT_DIMS) for p in n]
        u = [bdot(inv_s[slot[p]], stack(ls[p][:C] + t_s[slot[p]])) for p in n]
        u = [x.astype(BF16) for x in u]
        upd = [lax.dot_general(jnp.concatenate([u[p], v_s[slot[p]]], axis=0), dk_s[slot[p]], TN_DIMS,
                               preferred_element_type=F32) for p in n]
        for p in n:
            st_ref[p] = state[p] * dec_s[slot[p]] + jnp.where(same_head, upd[p], 0.0)
            u_s[slot[p]] = u[p]
            ls_s[slot[p]] = ls[p][C:]
        return 0

    def pass_c(i, _):
        prob = problems(i)
        n = range(len(prob))
        uv = [jnp.concatenate([stack(u_s[slot].astype(F32)), v2_s[slot]], axis=0)
              for slot, _, _, _ in prob]
        y = [ls_s[prob[j][0]] + jnp.dot(arbk_s[prob[j][0]], uv[j], preferred_element_type=F32)
             for j in n]
        mean = [head_sum(x) * np.float32(1.0 / nd) for x in y]
        d = [y[j] - mean[j] for j in n]
        var = [head_sum(x * x) * np.float32(1.0 / nd) for x in d]
        for j, (slot, rows, cols, _) in enumerate(prob):
            yn = d[j] * lax.rsqrt(var[j] + RWKV_GN_EPS) * lnw_ref[:, cols] + lnb_ref[:, cols]
            o_ref[rows, cols] = ((yn + bonus_s[slot]) * g_ref[rows, cols].astype(F32)).astype(o_ref.dtype)
        return 0

    lax.fori_loop(0, n_chunks // unroll, pass_a, 0)
    lax.fori_loop(0, n_chunks, pass_b, 0)
    lax.fori_loop(0, n_chunks // unroll, pass_c, 0)


def _wkv(r, wpre, k, v, a, g, v_first, v_gate, k_k, k_a, r_k, ln_w, ln_b, B, S):
    T, D = r.shape
    lanes = 2 * RWKV_HEAD_DIM
    has_vmix = v_first is not None
    n_pairs = next(n for n in (8, 4, 1) if D % (n * lanes) == 0)
    width = n_pairs * lanes
    ts = _pick_tile(S, 512, WKV_CHUNK)
    nt = S // ts
    seq = pl.BlockSpec((ts, width), lambda b, h, t: (b * nt + t, h))
    par = pl.BlockSpec((1, width), lambda b, h, t: (0, h))
    args = [r, wpre, k, v, a, g] + ([v_first, v_gate] if has_vmix else [])
    params = [p.reshape(1, D) for p in (k_k, k_a, r_k, ln_w, ln_b)]
    C = WKV_CHUNK
    n_chunks = ts // C
    slots = n_chunks * n_pairs
    scratch = [pltpu.VMEM((n_pairs, lanes, lanes), F32),
               pltpu.VMEM((slots, C, lanes), BF16),
               pltpu.VMEM((slots, C, lanes), F32),
               pltpu.VMEM((slots, C, 2 * lanes), BF16),
               pltpu.VMEM((slots, 2 * C, lanes), BF16),
               pltpu.VMEM((slots, 2 * C, lanes), BF16),
               pltpu.VMEM((slots, C, lanes), BF16),
               pltpu.VMEM((slots, 2 * C, lanes), BF16),
               pltpu.VMEM((slots, C, lanes), F32),
               pltpu.VMEM((slots, 1, lanes), F32),
               pltpu.VMEM((slots, C, lanes), BF16),
               pltpu.VMEM((slots, C, lanes), F32)]
    scratch_bytes = slots * C * lanes * 30
    return pl.pallas_call(
        functools.partial(_wkv_body, n_chunks=n_chunks, n_pairs=n_pairs, has_vmix=has_vmix),
        grid=(B, D // width, nt),
        in_specs=[seq] * len(args) + [par] * len(params),
        out_specs=seq,
        out_shape=jax.ShapeDtypeStruct((T, D), BF16),
        scratch_shapes=scratch,
        compiler_params=_cparams(3, 2 * ts * width * 4 * (len(args) + 1) + scratch_bytes
                                 + 16 * 1024 * 1024),
        name="wkv7_chunked",
    )(*args, *params)


def _pad_to(x, n, axis):
    pad = n - x.shape[axis]
    if pad == 0:
        return x
    widths = [(0, 0)] * x.ndim
    widths[axis] = (0, pad)
    return jnp.pad(x, widths)


def _lora(x, w_down, w_up, *, bias=None, act_mid=None, act_out=None, out_dtype=F32, name="lora"):
    inner = -(-w_down.shape[1] // LANE) * LANE
    wd = _pad_to(w_down, inner, 1).astype(BF16)
    wu = _pad_to(w_up, inner, 0).astype(BF16)
    mid = _matmul([(x, wd)], out_dtype=BF16, act=act_mid, name=name + "_down")
    return _matmul([(mid, wu)], out_dtype=out_dtype, bias=bias, act=act_out, name=name + "_up")


def _stacked(w):
    if isinstance(w, tuple):
        return w
    return w.astype(BF16)[None], 0


def _hybrid_layer(x, B, S, norm, w_in, q_norm, k_norm, lam_re, lam_im, log_dt, b_re, b_im, c_re, c_im,
                  d_skip, glu_w, glu_b, w_out):
    (w_in, l_in), (glu_w, l_glu), (w_out, l_out) = _stacked(w_in), _stacked(glu_w), _stacked(w_out)
    sw = d_skip.shape[0]
    aw = w_out.shape[1] - sw
    assert aw == sw
    n_heads = aw // ATTN_HEAD_DIM
    h = _rmsnorm(x, norm)
    qkvu = _matmul([(h, w_in, l_in, 0)], out_dtype=BF16, name="hybrid_in_proj")
    y_attn = _attention(qkvu, q_norm, k_norm, B, S, n_heads)
    ops = _s5_chunk_operators(lam_re, lam_im, log_dt, b_re, b_im, c_re, c_im)
    z = _s5(qkvu[:, 3 * aw:], ops, d_skip, B, S)
    y_ssm = _matmul([(z, glu_w, l_glu, 0)], out_dtype=BF16, bias=glu_b, act="sigmoid", mul=z,
                    name="ssm_glu")
    return _matmul([(y_attn, w_out, l_out, 0), (y_ssm, w_out, l_out, 1)], out_dtype=F32, residual=x,
                   name="hybrid_out_proj")


def _rwkv_layer(x, B, S, norm, mu, w_r, w_k, w_v, w0, w1, w2, a0, a1, a2, g1, g2, k_k, k_a, r_k,
                ln_w, ln_b, w_o, v_first, v_mix):
    l_r, l_w, l_k, l_v, l_a, l_g = _rwkv_lerp(x, norm, mu, S)
    r = _matmul([(l_r, *_stacked(w_r), 0)], out_dtype=F32, name="rwkv_r")
    k = _matmul([(l_k, *_stacked(w_k), 0)], out_dtype=F32, name="rwkv_k")
    v = _matmul([(l_v, *_stacked(w_v), 0)], out_dtype=F32, name="rwkv_v")
    wpre = _lora(l_w, w1, w2, bias=w0, act_mid="tanh", name="rwkv_decay")
    a = _lora(l_a, a1, a2, bias=a0, act_out="sigmoid", name="rwkv_iclr")
    g = _lora(l_g, g1, g2, act_mid="sigmoid", out_dtype=BF16, name="rwkv_gate")
    if v_mix is None:
        v_gate, vf = None, None
        v_first = v
    else:
        v0, v1, v2 = v_mix
        v_gate = _lora(l_v, v1, v2, bias=v0, act_out="sigmoid", name="rwkv_vmix")
        vf = v_first
    y = _wkv(r, wpre, k, v, a, g, vf, v_gate, k_k, k_a, r_k.reshape(-1), ln_w, ln_b, B, S)
    return _matmul([(y, *_stacked(w_o), 0)], out_dtype=F32, residual=x, name="rwkv_out_proj"), v_first


def _ffn_layer(x, B, S, norm, w_up, conv_w, conv_b, w_down):
    h = _rmsnorm(x, norm)
    act = _ffn_up_act(h, *_stacked(w_up), conv_w, conv_b, S)
    return _matmul([(act, *_stacked(w_down), 0)], out_dtype=F32, residual=x, name="ffn_down")


def kernel(x, mix_norm, hy_w_in, attn_q_norm, attn_k_norm, ssm_lambda_re, ssm_lambda_im, ssm_log_dt, ssm_b_re, ssm_b_im, ssm_c_re, ssm_c_im, ssm_d, ssm_glu_w, ssm_glu_b, hy_w_out, rwkv_mu, rwkv_w_r, rwkv_w_k, rwkv_w_v, rwkv_w0, rwkv_w1, rwkv_w2, rwkv_a0, rwkv_a1, rwkv_a2, rwkv_g1, rwkv_g2, rwkv_k_k, rwkv_k_a, rwkv_r_k, rwkv_ln_w, rwkv_ln_b, rwkv_w_o, rwkv_v0, rwkv_v1, rwkv_v2, ffn_norm, ffn_w_up, ffn_conv_w, ffn_conv_b, ffn_w_down):
    B, S, D = x.shape
    depth = mix_norm.shape[0]
    x = x.reshape(B * S, D)
    hy_w_in, ssm_glu_w, hy_w_out, rwkv_w_r, rwkv_w_k, rwkv_w_v, rwkv_w_o, ffn_w_up, ffn_w_down = (
        w.astype(BF16) for w in (hy_w_in, ssm_glu_w, hy_w_out, rwkv_w_r, rwkv_w_k, rwkv_w_v,
                                 rwkv_w_o, ffn_w_up, ffn_w_down))
    v_first = None
    for layer in range(depth):
        i = layer // 2
        if layer % 2 == 0:
            x = _hybrid_layer(x, B, S, mix_norm[layer], (hy_w_in, i), attn_q_norm[i], attn_k_norm[i],
                              ssm_lambda_re[i], ssm_lambda_im[i], ssm_log_dt[i], ssm_b_re[i],
                              ssm_b_im[i], ssm_c_re[i], ssm_c_im[i], ssm_d[i], (ssm_glu_w, i),
                              ssm_glu_b[i], (hy_w_out, i))
        else:
            v_mix = None if i == 0 else (rwkv_v0[i - 1], rwkv_v1[i - 1], rwkv_v2[i - 1])
            x, v_first = _rwkv_layer(x, B, S, mix_norm[layer], rwkv_mu[i], (rwkv_w_r, i), (rwkv_w_k, i),
                                     (rwkv_w_v, i), rwkv_w0[i], rwkv_w1[i], rwkv_w2[i], rwkv_a0[i],
                                     rwkv_a1[i], rwkv_a2[i], rwkv_g1[i], rwkv_g2[i], rwkv_k_k[i],
                                     rwkv_k_a[i], rwkv_r_k[i], rwkv_ln_w[i], rwkv_ln_b[i],
                                     (rwkv_w_o, i), v_first, v_mix)
        x = _ffn_layer(x, B, S, ffn_norm[layer], (ffn_w_up, layer), ffn_conv_w[layer],
                       ffn_conv_b[layer], (ffn_w_down, layer))
    return x.reshape(B, S, D)
```

```python
import functools
import math

import numpy as np
import jax
import jax.numpy as jnp
from jax import lax
from jax.experimental import pallas as pl
from jax.experimental.pallas import tpu as pltpu

F32 = jnp.float32
BF16 = jnp.bfloat16
HIGHEST = lax.Precision.HIGHEST

RMS_EPS = 1e-6
ATTN_HEAD_DIM = 128
DILATED_BRANCHES = ((128, 1), (512, 4), (2048, 16))
ROPE_THETA = 10000.0
SSM_GROUP = 16
SSM_STATE = 64
SSM_CHUNK = 16
RWKV_HEAD_DIM = 64
RWKV_GN_EPS = 64e-5
WKV_CHUNK = 64
FFN_CONV = 3

V7X_VMEM_BYTES = 64 * 1024 * 1024
VMEM_BUDGET = 48 * 1024 * 1024
LANE = 128
SUBLANE = 8

NT_DIMS = (((1,), (1,)), ((), ()))
TN_DIMS = (((0,), (0,)), ((), ()))


def _cparams(n_grid, vmem_bytes):
    limit = int(min(max(vmem_bytes, 16 * 1024 * 1024), V7X_VMEM_BYTES - 6 * 1024 * 1024))
    return pltpu.CompilerParams(dimension_semantics=("arbitrary",) * n_grid,
                                vmem_limit_bytes=limit)


def _pick_tile(n, pref, align):
    t = min(pref, n)
    t -= t % align
    while t >= align:
        if n % t == 0:
            return t
        t -= align
    return n


def _sigmoid(x):
    return 1.0 / (1.0 + jnp.exp(-x))


def _gelu(x):
    return 0.5 * x * (1.0 + lax.erf(x * np.float32(math.sqrt(0.5))))


def _rms(x, gain):
    return x * lax.rsqrt(jnp.mean(x * x, axis=-1, keepdims=True) + RMS_EPS) * gain


def _rmsnorm_body(x_ref, g_ref, o_ref):
    o_ref[...] = _rms(x_ref[...], g_ref[...]).astype(o_ref.dtype)


def _rmsnorm(x, gain, out_dtype=BF16):
    T, D = x.shape
    tm = _pick_tile(T, 256, SUBLANE)
    return pl.pallas_call(
        _rmsnorm_body,
        grid=(T // tm,),
        in_specs=[pl.BlockSpec((tm, D), lambda i: (i, 0)),
                  pl.BlockSpec((1, D), lambda i: (0, 0))],
        out_specs=pl.BlockSpec((tm, D), lambda i: (i, 0)),
        out_shape=jax.ShapeDtypeStruct((T, D), out_dtype),
        compiler_params=_cparams(1, 6 * tm * D * 4),
        name="rmsnorm",
    )(x, gain.reshape(1, D))


def _lerp_body(x_ref, xh_ref, g_ref, mu_ref, *o_refs, tm, seq):
    i = pl.program_id(0)
    g = g_ref[...]
    h = _rms(x_ref[...], g)
    hh = _rms(xh_ref[...], g)
    prev = jnp.where((i * tm) % seq == 0, 0.0, hh[SUBLANE - 1:SUBLANE, :])
    hs = pltpu.roll(h, 1, axis=0)
    row = lax.broadcasted_iota(jnp.int32, h.shape, 0)
    xx = jnp.where(row == 0, prev, hs) - h
    for n, o_ref in enumerate(o_refs):
        o_ref[...] = (h + xx * mu_ref[n:n + 1, :]).astype(o_ref.dtype)


def _rwkv_lerp(x, gain, mu, seq):
    T, D = x.shape
    n = mu.shape[0]
    tm = _pick_tile(seq, 128, SUBLANE)
    hb = tm // SUBLANE
    return pl.pallas_call(
        functools.partial(_lerp_body, tm=tm, seq=seq),
        grid=(T // tm,),
        in_specs=[pl.BlockSpec((tm, D), lambda i: (i, 0)),
                  pl.BlockSpec((SUBLANE, D), lambda i: (jnp.maximum(i * hb - 1, 0), 0)),
                  pl.BlockSpec((1, D), lambda i: (0, 0)),
                  pl.BlockSpec((n, D), lambda i: (0, 0))],
        out_specs=[pl.BlockSpec((tm, D), lambda i: (i, 0))] * n,
        out_shape=[jax.ShapeDtypeStruct((T, D), BF16)] * n,
        compiler_params=_cparams(1, tm * D * (2 * 4 + 2 * 2 * n + 4 * 4)),
        name="rwkv_lerp",
    )(x, x, gain.reshape(1, D), mu)


def _mm_body(*refs, n_pairs, has_bias, act, has_mul, has_res):
    acc = None
    idx = 0
    for _ in range(n_pairs):
        d = jnp.dot(refs[idx][...], refs[idx + 1][...], preferred_element_type=F32)
        acc = d if acc is None else acc + d
        idx += 2
    if has_bias:
        acc = acc + refs[idx][...]
        idx += 1
    if act == "tanh":
        acc = jnp.tanh(acc)
    elif act == "sigmoid":
        acc = _sigmoid(acc)
    if has_mul:
        acc = acc * refs[idx][...].astype(F32)
        idx += 1
    if has_res:
        acc = acc + refs[idx][...]
        idx += 1
    o_ref = refs[idx]
    o_ref[...] = acc.astype(o_ref.dtype)


def _mm_tiles(M, N, k_total, out_bytes, extra_bytes):
    for tm_pref, tn_pref in ((1024, 512), (512, 512), (512, 256), (256, 256), (128, 256), (128, 128)):
        tm = _pick_tile(M, tm_pref, 16)
        tn = _pick_tile(N, tn_pref, LANE)
        need = (2 * tm * k_total * 2 + 2 * k_total * tn * 2
                + tm * tn * (2 * out_bytes + 2 * extra_bytes + 8))
        if need <= VMEM_BUDGET:
            return tm, tn, need
    return tm, tn, need


def _matmul(pairs, *, out_dtype, bias=None, act=None, mul=None, residual=None, name="matmul"):
    M = pairs[0][0].shape[0]
    N = pairs[0][1].shape[-1]
    k_total = sum(p[0].shape[1] for p in pairs)
    out_bytes = jnp.dtype(out_dtype).itemsize
    extra = (2 if mul is not None else 0) + (4 if residual is not None else 0)
    tm, tn, need = _mm_tiles(M, N, k_total, out_bytes, extra)
    args, in_specs = [], []
    for pair in pairs:
        a, w = pair[0], pair[1]
        layer, kblock = (pair[2], pair[3]) if len(pair) == 4 else (0, 0)
        if w.ndim == 2:
            w = w[None]
        K = a.shape[1]
        assert a.dtype == BF16 and w.dtype == BF16 and a.shape[0] == M
        assert w.shape[2] == N and w.shape[1] % K == 0
        args += [a, w]
        in_specs += [pl.BlockSpec((tm, K), lambda i, j: (i, 0)),
                     pl.BlockSpec((None, K, tn), lambda i, j, layer=layer, kblock=kblock: (layer, kblock, j))]
    if bias is not None:
        args.append(bias.reshape(1, N).astype(F32))
        in_specs.append(pl.BlockSpec((1, tn), lambda i, j: (0, j)))
    if mul is not None:
        args.append(mul)
        in_specs.append(pl.BlockSpec((tm, tn), lambda i, j: (i, j)))
    if residual is not None:
        args.append(residual)
        in_specs.append(pl.BlockSpec((tm, tn), lambda i, j: (i, j)))
    return pl.pallas_call(
        functools.partial(_mm_body, n_pairs=len(pairs), has_bias=bias is not None, act=act,
                          has_mul=mul is not None, has_res=residual is not None),
        grid=(M // tm, N // tn),
        in_specs=in_specs,
        out_specs=pl.BlockSpec((tm, tn), lambda i, j: (i, j)),
        out_shape=jax.ShapeDtypeStruct((M, N), out_dtype),
        compiler_params=_cparams(2, need + 8 * 1024 * 1024),
        name=name,
    )(*args)


def _attn_bias(S, tq):
    delta = (S - tq) + np.arange(tq)[:, None] - np.arange(S)[None, :]
    mult = np.zeros(delta.shape, np.float64)
    for window, dilation in DILATED_BRANCHES:
        mult += (delta >= 0) & (delta <= window) & (delta % dilation == 0)
    bias = np.where(mult > 0, np.log(np.maximum(mult, 1.0)), -1e30)
    return jnp.asarray(bias, F32)


def _rope_tables(S):
    half = ATTN_HEAD_DIM // 2
    inv_freq = ROPE_THETA ** (-jnp.arange(half, dtype=F32) / half)
    ang = jnp.arange(S, dtype=jnp.int32).astype(F32)[:, None] * inv_freq[None, :]
    cos, sin = jnp.cos(ang), jnp.sin(ang)
    return jnp.concatenate([cos, cos], axis=-1), jnp.concatenate([-sin, sin], axis=-1)


def _attn_body(q_ref, k_ref, v_ref, qg_ref, kg_ref, cos_ref, sin_ref, bias_ref, o_ref,
               qs_ref, ks_ref, *, S, tq):
    half = ATTN_HEAD_DIM // 2

    def prep(x_ref, g_ref, scale):
        x = _rms(x_ref[...].astype(F32), g_ref[...])
        x = x * cos_ref[...] + pltpu.roll(x, half, axis=1) * sin_ref[...]
        return (x * scale).astype(BF16)

    qs_ref[...] = prep(q_ref, qg_ref, np.float32(ATTN_HEAD_DIM ** -0.5))
    ks_ref[...] = prep(k_ref, kg_ref, np.float32(1.0))

    n_blocks = S // tq

    def scores(i):
        return lax.dot_general(qs_ref[i * tq:(i + 1) * tq, :], ks_ref[:(i + 1) * tq, :], NT_DIMS,
                               preferred_element_type=F32)

    s_next = scores(0)
    for i in range(n_blocks):
        s = s_next + bias_ref[:, (n_blocks - 1 - i) * tq:]
        if i + 1 < n_blocks:
            s_next = scores(i + 1)
        p = jnp.exp(s - jnp.max(s, axis=-1, keepdims=True))
        l = jnp.sum(p, axis=-1, keepdims=True)
        acc = jnp.dot(p.astype(BF16), v_ref[:(i + 1) * tq, :], preferred_element_type=F32)
        o_ref[i * tq:(i + 1) * tq, :] = (acc / l).astype(o_ref.dtype)


def _attention(qkvu, q_gain, k_gain, B, S, n_heads):
    T = B * S
    dh = ATTN_HEAD_DIM
    tq = _pick_tile(S, 256, LANE)
    cos, sin = _rope_tables(S)
    bias = _attn_bias(S, tq)
    head = lambda off: pl.BlockSpec((S, dh), lambda b, h: (b, off + h))
    const2 = lambda shape: pl.BlockSpec(shape, lambda b, h: (0, 0))
    return pl.pallas_call(
        functools.partial(_attn_body, S=S, tq=tq),
        grid=(B, n_heads),
        in_specs=[head(0), head(n_heads), head(2 * n_heads),
                  const2((1, dh)), const2((1, dh)), const2((S, dh)), const2((S, dh)),
                  const2(bias.shape)],
        out_specs=pl.BlockSpec((S, dh), lambda b, h: (b, h)),
        out_shape=jax.ShapeDtypeStruct((T, n_heads * dh), BF16),
        scratch_shapes=[pltpu.VMEM((S, dh), BF16), pltpu.VMEM((S, dh), BF16)],
        compiler_params=_cparams(2, 2 * bias.size * 4 + 24 * S * dh * 4),
        name="dilated_attention",
    )(qkvu, qkvu, qkvu, q_gain.reshape(1, dh), k_gain.reshape(1, dh), cos, sin, bias)


def _s5_chunk_operators(lam_re, lam_im, log_dt, b_re, b_im, c_re, c_im):
    Lc = SSM_CHUNK
    G, N = lam_re.shape
    P = b_re.shape[-1]
    dt = jnp.exp(log_dt)[:, None]
    mag = jnp.exp(lam_re * dt)
    ab_re, ab_im = mag * jnp.cos(lam_im * dt), mag * jnp.sin(lam_im * dt)
    den = lam_re * lam_re + lam_im * lam_im
    nr, ni = ab_re - 1.0, ab_im
    coef_re = (nr * lam_re + ni * lam_im) / den
    coef_im = (ni * lam_re - nr * lam_im) / den
    bb_re = coef_re[..., None] * b_re - coef_im[..., None] * b_im
    bb_im = coef_re[..., None] * b_im + coef_im[..., None] * b_re
    steps = jnp.arange(Lc + 1, dtype=F32)[:, None, None]
    pmag = jnp.exp(lam_re * dt * steps)
    pw_re, pw_im = pmag * jnp.cos(lam_im * dt * steps), pmag * jnp.sin(lam_im * dt * steps)
    ca_re = c_re[None] * pw_re[:Lc, :, None, :] - c_im[None] * pw_im[:Lc, :, None, :]
    ca_im = c_re[None] * pw_im[:Lc, :, None, :] + c_im[None] * pw_re[:Lc, :, None, :]
    lag = (jnp.sum(ca_re[..., None] * bb_re[None, :, None], axis=3)
           - jnp.sum(ca_im[..., None] * bb_im[None, :, None], axis=3))
    rev = Lc - 1 - jnp.arange(Lc)
    bin_re = pw_re[rev][..., None] * bb_re[None] - pw_im[rev][..., None] * bb_im[None]
    bin_im = pw_re[rev][..., None] * bb_im[None] + pw_im[rev][..., None] * bb_re[None]
    nxt_re, nxt_im = pw_re[1:], pw_im[1:]
    co_re = c_re[None] * nxt_re[:, :, None, :] - c_im[None] * nxt_im[:, :, None, :]
    co_im = c_re[None] * nxt_im[:, :, None, :] + c_im[None] * nxt_re[:, :, None, :]

    gb = LANE // P
    nb = G // gb

    def spread(x):
        R, C = x.shape[2:]
        x = x.reshape(Lc, nb, gb, R, C).transpose(1, 0, 2, 3, 4).reshape(nb, Lc, gb * R, C)
        copies = jnp.tile(jnp.eye(C, dtype=BF16), (1, gb))
        y = jnp.dot(x.astype(BF16), copies, preferred_element_type=F32)
        same = (jnp.arange(gb * R)[:, None] // R) == (jnp.arange(gb * C)[None, :] // C)
        return jnp.where(same, y, 0.0).astype(BF16)

    lag8 = spread(lag.transpose(0, 1, 3, 2))
    bin8_re = spread(bin_re.transpose(0, 1, 3, 2)).reshape(nb, Lc * LANE, gb * N)
    bin8_im = spread(bin_im.transpose(0, 1, 3, 2)).reshape(nb, Lc * LANE, gb * N)
    cout8_re = spread(co_re.transpose(0, 1, 3, 2))
    cout8_im = spread((-co_im).transpose(0, 1, 3, 2))
    return (lag8, bin8_re, bin8_im, cout8_re, cout8_im,
            pw_re[Lc].reshape(nb, 1, gb * N), pw_im[Lc].reshape(nb, 1, gb * N))


def _s5_body(u_ref, lag_ref, bre_ref, bim_ref, cre_ref, cim_ref, are_ref, aim_ref, d_ref, z_ref,
             toep_s, cout_re_s, cout_im_s, xin_re, xin_im, xst_re, xst_im, car_re, car_im,
             *, n_chunks, batch):
    Lc = SSM_CHUNK

    @pl.when(pl.program_id(1) == 0)
    def _():
        car_re[...] = jnp.zeros_like(car_re)
        car_im[...] = jnp.zeros_like(car_im)
        toep_s[...] = jnp.zeros_like(toep_s)
        for ti in range(Lc):
            for to in range(ti, Lc):
                toep_s[ti * LANE:(ti + 1) * LANE, to * LANE:(to + 1) * LANE] = lag_ref[0, to - ti]
        for to in range(Lc):
            cout_re_s[:, to * LANE:(to + 1) * LANE] = cre_ref[0, to]
            cout_im_s[:, to * LANE:(to + 1) * LANE] = cim_ref[0, to]

    u = u_ref[0]
    xin_re[...] = jnp.dot(u, bre_ref[0], preferred_element_type=F32)
    xin_im[...] = jnp.dot(u, bim_ref[0], preferred_element_type=F32)
    a_re, a_im = are_ref[0], aim_ref[0]

    def step(c, carry):
        x_re, x_im = carry
        rows = pl.ds(pl.multiple_of(c * batch, batch), batch)
        xst_re[rows, :] = x_re
        xst_im[rows, :] = x_im
        return (a_re * x_re - a_im * x_im + xin_re[rows, :],
                a_re * x_im + a_im * x_re + xin_im[rows, :])

    x_re, x_im = lax.fori_loop(0, n_chunks, step, (car_re[...], car_im[...]))
    car_re[...] = x_re
    car_im[...] = x_im
    y = (jnp.dot(u, toep_s[...], preferred_element_type=F32)
         + jnp.dot(xst_re[...].astype(BF16), cout_re_s[...], preferred_element_type=F32)
         + jnp.dot(xst_im[...].astype(BF16), cout_im_s[...], preferred_element_type=F32))
    y = y + d_ref[0] * u.astype(F32)
    z_ref[0] = _gelu(y).astype(z_ref.dtype)


def _s5(u, ops, d_skip, B, S):
    lag8, bin_re, bin_im, cout_re, cout_im, a_re, a_im = ops
    nb, W, N = bin_re.shape
    Lc = SSM_CHUNK
    nc = S // Lc
    parts = 4 if nc % 4 == 0 else 1
    rows = (nc // parts) * B
    ut = u.reshape(B, nc, Lc, nb, LANE).transpose(3, 1, 0, 2, 4).reshape(nb, nc * B, W)
    d_t = jnp.tile(d_skip.reshape(nb, 1, LANE), (1, Lc, 1)).reshape(nb, 1, W)
    const = lambda shape: pl.BlockSpec((1,) + shape, lambda l, r: (l,) + (0,) * len(shape))
    block = pl.BlockSpec((1, rows, W), lambda l, r: (l, r, 0))
    zt = pl.pallas_call(
        functools.partial(_s5_body, n_chunks=nc // parts, batch=B),
        grid=(nb, parts),
        in_specs=[block, const((Lc, LANE, LANE)), const((W, N)), const((W, N)),
                  const((Lc, N, LANE)), const((Lc, N, LANE)), const((1, N)), const((1, N)),
                  const((1, W))],
        out_specs=block,
        out_shape=jax.ShapeDtypeStruct((nb, nc * B, W), BF16),
        scratch_shapes=([pltpu.VMEM((W, W), BF16), pltpu.VMEM((N, W), BF16), pltpu.VMEM((N, W), BF16)]
                        + [pltpu.VMEM((rows, N), F32)] * 4 + [pltpu.VMEM((B, N), F32)] * 2),
        compiler_params=_cparams(2, (W * W + 10 * W * N) * 2 + rows * W * 24 + rows * N * 32),
        name="s5_chunked_scan",
    )(ut, lag8, bin_re, bin_im, cout_re, cout_im, a_re, a_im, d_t)
    return zt.reshape(nb, nc, B, Lc, LANE).transpose(2, 1, 3, 0, 4).reshape(B * S, nb * LANE)


def _ffn_up_body(h_ref, wg_ref, wv_ref, cw_ref, cb_ref, o_ref, gate_ref, val_ref, tail_ref,
                 *, tm, seq, nj):
    i, j = pl.program_id(0), pl.program_id(1)

    def conv(g, shifted):
        y = cw_ref[FFN_CONV - 1:FFN_CONV, :] * g + cb_ref[...]
        for back in range(1, FFN_CONV):
            y = y + cw_ref[FFN_CONV - 1 - back:FFN_CONV - back, :] * shifted(back)
        return y

    def activate():
        jj = j - 1
        gate, val = gate_ref[...], val_ref[...]
        y = conv(gate, lambda back: pltpu.roll(gate, back, axis=0))
        o_ref[...] = (_gelu(y) * val).astype(o_ref.dtype)
        prev = jnp.where((i * tm) % seq == 0, 0.0, tail_ref[jj])
        top = gate[:SUBLANE, :]
        row = lax.broadcasted_iota(jnp.int32, top.shape, 0)
        y_top = conv(top, lambda back: jnp.where(row < back, pltpu.roll(prev, back, axis=0),
                                                 pltpu.roll(top, back, axis=0)))
        o_ref[:SUBLANE, :] = (_gelu(y_top) * val[:SUBLANE, :]).astype(o_ref.dtype)
        tail_ref[jj] = gate[tm - SUBLANE:, :]

    def multiply():
        h = h_ref[...]
        gate_ref[...] = jnp.dot(h, wg_ref[...], preferred_element_type=F32)
        val_ref[...] = jnp.dot(h, wv_ref[...], preferred_element_type=F32)

    @pl.when(jnp.logical_and(i == 0, j == 0))
    def _():
        tail_ref[...] = jnp.zeros_like(tail_ref)

    @pl.when(j == 0)
    def _():
        multiply()

    @pl.when(jnp.logical_and(j > 0, j < nj))
    def _():
        activate()
        multiply()

    @pl.when(j == nj)
    def _():
        activate()


def _ffn_up_act(h, w_up, layer, conv_w, conv_b, seq):
    T, D = h.shape
    Fd = w_up.shape[2] // 2
    tm = _pick_tile(seq, 1024, 16)
    tn = _pick_tile(Fd, 512, LANE)
    nj = Fd // tn
    cur = lambda j: jnp.minimum(j, nj - 1)
    lag = lambda j: jnp.maximum(j - 1, 0)
    need = 2 * tm * D * 2 + 4 * D * tn * 2 + 2 * tm * tn * 2 + 12 * tm * tn * 4
    return pl.pallas_call(
        functools.partial(_ffn_up_body, tm=tm, seq=seq, nj=nj),
        grid=(T // tm, nj + 1),
        in_specs=[pl.BlockSpec((tm, D), lambda i, j: (i, 0)),
                  pl.BlockSpec((None, D, tn), lambda i, j: (layer, 0, cur(j))),
                  pl.BlockSpec((None, D, tn), lambda i, j: (layer, 0, nj + cur(j))),
                  pl.BlockSpec((FFN_CONV, tn), lambda i, j: (0, lag(j))),
                  pl.BlockSpec((1, tn), lambda i, j: (0, lag(j)))],
        out_specs=pl.BlockSpec((tm, tn), lambda i, j: (i, lag(j))),
        out_shape=jax.ShapeDtypeStruct((T, Fd), BF16),
        scratch_shapes=[pltpu.VMEM((tm, tn), F32), pltpu.VMEM((tm, tn), F32),
                        pltpu.VMEM((nj, SUBLANE, tn), F32)],
        compiler_params=_cparams(2, need),
        name="ffn_up_convglu",
    )(h, w_up, w_up, conv_w, conv_b.reshape(1, Fd))


def _wkv_body(*refs, n_chunks, n_pairs, has_vmix):
    C = WKV_CHUNK
    nd = RWKV_HEAD_DIM
    lanes = 2 * nd
    n_seq = 8 if has_vmix else 6
    r_ref, w_ref, k_ref, v_ref, a_ref, g_ref = refs[:6]
    vf_ref, vg_ref = refs[6:8] if has_vmix else (None, None)
    kk_ref, ka_ref, rk_ref, lnw_ref, lnb_ref, o_ref = refs[n_seq:n_seq + 6]
    (st_ref, inv_s, t_s, arbk_s, left_s, dk_s, v_s, v2_s, bonus_s, dec_s, u_s, ls_s) = refs[n_seq + 6:]
    unroll = 2 if n_chunks % 2 == 0 else 1

    @pl.when(pl.program_id(2) == 0)
    def _():
        st_ref[...] = jnp.zeros_like(st_ref)

    lane_r = lax.broadcasted_iota(jnp.int32, (lanes, lanes), 0)
    lane_c = lax.broadcasted_iota(jnp.int32, (lanes, lanes), 1)
    same_head = (lane_r // nd) == (lane_c // nd)
    head_ones = same_head.astype(BF16)
    pos_r = lax.broadcasted_iota(jnp.int32, (C, lanes), 0)
    pos_c = lax.broadcasted_iota(jnp.int32, (C, lanes), 1) % C
    strict = pos_r > pos_c
    incl = pos_r >= pos_c
    eye = (pos_r == pos_c).astype(F32)
    tri = (lax.broadcasted_iota(jnp.int32, (C, C), 0)
           >= lax.broadcasted_iota(jnp.int32, (C, C), 1)).astype(BF16)
    lane1 = lax.broadcasted_iota(jnp.int32, (1, lanes), 1)
    m0 = (lane1 < nd).astype(F32)
    m1 = 1.0 - m0

    def split(x):
        hi = x.astype(BF16)
        return hi, (x - hi.astype(F32)).astype(BF16)

    def head_sum(x):
        n = x.shape[0]
        z = jnp.dot(jnp.concatenate(split(x), axis=0), head_ones, preferred_element_type=F32)
        return z[:n] + z[n:]

    def stack(x):
        return jnp.concatenate([x * m0, x * m1], axis=0).astype(BF16)

    def bdot(a, b, dims=None):
        a, b = a.astype(BF16), b.astype(BF16)
        if dims is None:
            return jnp.dot(a, b, preferred_element_type=F32)
        return lax.dot_general(a, b, dims, preferred_element_type=F32)

    def problems(i):
        out = []
        for cc in range(unroll):
            c = i * unroll + cc
            for p in range(n_pairs):
                out.append((c * n_pairs + p, pl.ds(pl.multiple_of(c * C, C), C),
                            slice(p * lanes, (p + 1) * lanes), p))
        return out

    def pass_a(i, _):
        prob = problems(i)
        ld = lambda ref: [ref[rows, cols] for _, rows, cols, _ in prob]
        par = lambda ref: [ref[:, cols] for _, _, cols, _ in prob]
        n = range(len(prob))
        r, k, v, a, wpre = ld(r_ref), ld(k_ref), ld(v_ref), ld(a_ref), ld(w_ref)
        k_k, k_a, r_k = par(kk_ref), par(ka_ref), par(rk_ref)
        if has_vmix:
            vf, vg = ld(vf_ref), ld(vg_ref)
            v = [v[j] + (vf[j] - v[j]) * vg[j] for j in n]
        dlog = [-jnp.exp(-(jnp.maximum(-w, 0.0) + jnp.log(1.0 + jnp.exp(-jnp.abs(w)))) - 0.5)
                for w in wpre]
        kk = [k[j] * k_k[j] for j in n]
        k2 = [k[j] * (1.0 + (a[j] - 1.0) * k_a[j]) for j in n]
        sums = [head_sum(jnp.concatenate([kk[j] * kk[j], r[j] * k2[j] * r_k[j]], axis=0)) for j in n]
        z = [jnp.dot(tri, jnp.concatenate(split(dlog[j]), axis=1), preferred_element_type=F32)
             for j in n]
        lg = [zz[:, :lanes] + zz[:, lanes:] for zz in z]
        kk = [kk[j] / jnp.maximum(jnp.sqrt(sums[j][:C]), 1e-12) for j in n]
        bv = [kk[j] * a[j] for j in n]
        e_neg = [jnp.exp(-l) for l in lg]
        e_end = [jnp.exp(l[C - 1:C, :] - l) for l in lg]
        left = [jnp.concatenate([-kk[j] * jnp.exp(lg[j] - dlog[j]), r[j] * jnp.exp(lg[j])],
                                axis=0).astype(BF16) for j in n]
        right = [jnp.concatenate([stack(bv[j] * e_neg[j]), stack(k2[j] * e_neg[j])], axis=0)
                 for j in n]
        aa = [bdot(left[j], right[j], NT_DIMS) for j in n]
        a_ab = [jnp.where(strict, m[:C, :lanes], 0.0) for m in aa]
        v2 = [stack(x) for x in v]
        t = [bdot(jnp.where(strict, aa[j][:C, lanes:], 0.0), v2[j]) for j in n]
        power = [bdot(m, stack(m)) for m in a_ab]
        inv = [eye + m for m in a_ab]
        for _ in range(2, int(math.log2(C))):
            z = [bdot(jnp.concatenate([inv[j], power[j]], axis=0), stack(power[j])) for j in n]
            inv = [inv[j] + z[j][:C] for j in n]
            power = [zz[C:] for zz in z]
        inv = [inv[j] + bdot(inv[j], stack(power[j])) for j in n]
        for j, (slot, rows, cols, _) in enumerate(prob):
            inv_s[slot] = inv[j].astype(BF16)
            t_s[slot] = t[j]
            arbk_s[slot] = jnp.where(jnp.concatenate([incl, incl], axis=1), aa[j][C:], 0.0).astype(BF16)
            left_s[slot] = left[j]
            dk_s[slot] = jnp.concatenate([bv[j] * e_end[j], k2[j] * e_end[j]], axis=0).astype(BF16)
            v_s[slot] = v[j].astype(BF16)
            v2_s[slot] = v2[j]
            bonus_s[slot] = sums[j][C:] * v[j]
            dec_s[slot] = jnp.exp(lg[j][C - 1:C, :])
        return 0

    def pass_b(c, _):
        n = range(n_pairs)
        slot = [c * n_pairs + p for p in n]
        state = [st_ref[p] for p in n]
        ls = [bdot(left_s[slot[p]], state[p], NT_DIMS) for p in n]
        u = [bdot(inv_s[slot[p]], stack(ls[p][:C] + t_s[slot[p]])) for p in n]
        u = [x.astype(BF16) for x in u]
        upd = [lax.dot_general(jnp.concatenate([u[p], v_s[slot[p]]], axis=0), dk_s[slot[p]], TN_DIMS,
                               preferred_element_type=F32) for p in n]
        for p in n:
            st_ref[p] = state[p] * dec_s[slot[p]] + jnp.where(same_head, upd[p], 0.0)
            u_s[slot[p]] = u[p]
            ls_s[slot[p]] = ls[p][C:]
        return 0

    def pass_c(i, _):
        prob = problems(i)
        n = range(len(prob))
        uv = [jnp.concatenate([stack(u_s[slot].astype(F32)), v2_s[slot]], axis=0)
              for slot, _, _, _ in prob]
        y = [ls_s[prob[j][0]] + jnp.dot(arbk_s[prob[j][0]], uv[j], preferred_element_type=F32)
             for j in n]
        mean = [head_sum(x) * np.float32(1.0 / nd) for x in y]
        d = [y[j] - mean[j] for j in n]
        var = [head_sum(x * x) * np.float32(1.0 / nd) for x in d]
        for j, (slot, rows, cols, _) in enumerate(prob):
            yn = d[j] * lax.rsqrt(var[j] + RWKV_GN_EPS) * lnw_ref[:, cols] + lnb_ref[:, cols]
            o_ref[rows, cols] = ((yn + bonus_s[slot]) * g_ref[rows, cols].astype(F32)).astype(o_ref.dtype)
        return 0

    lax.fori_loop(0, n_chunks // unroll, pass_a, 0)
    lax.fori_loop(0, n_chunks, pass_b, 0)
    lax.fori_loop(0, n_chunks // unroll, pass_c, 0)


def _wkv(r, wpre, k, v, a, g, v_first, v_gate, k_k, k_a, r_k, ln_w, ln_b, B, S):
    T, D = r.shape
    lanes = 2 * RWKV_HEAD_DIM
    has_vmix = v_first is not None
    n_pairs = next(n for n in (8, 4, 1) if D % (n * lanes) == 0)
    width = n_pairs * lanes
    ts = _pick_tile(S, 512, WKV_CHUNK)
    nt = S // ts
    seq = pl.BlockSpec((ts, width), lambda b, h, t: (b * nt + t, h))
    par = pl.BlockSpec((1, width), lambda b, h, t: (0, h))
    args = [r, wpre, k, v, a, g] + ([v_first, v_gate] if has_vmix else [])
    params = [p.reshape(1, D) for p in (k_k, k_a, r_k, ln_w, ln_b)]
    C = WKV_CHUNK
    n_chunks = ts // C
    slots = n_chunks * n_pairs
    scratch = [pltpu.VMEM((n_pairs, lanes, lanes), F32),
               pltpu.VMEM((slots, C, lanes), BF16),
               pltpu.VMEM((slots, C, lanes), F32),
               pltpu.VMEM((slots, C, 2 * lanes), BF16),
               pltpu.VMEM((slots, 2 * C, lanes), BF16),
               pltpu.VMEM((slots, 2 * C, lanes), BF16),
               pltpu.VMEM((slots, C, lanes), BF16),
               pltpu.VMEM((slots, 2 * C, lanes), BF16),
               pltpu.VMEM((slots, C, lanes), F32),
               pltpu.VMEM((slots, 1, lanes), F32),
               pltpu.VMEM((slots, C, lanes), BF16),
               pltpu.VMEM((slots, C, lanes), F32)]
    scratch_bytes = slots * C * lanes * 30
    return pl.pallas_call(
        functools.partial(_wkv_body, n_chunks=n_chunks, n_pairs=n_pairs, has_vmix=has_vmix),
        grid=(B, D // width, nt),
        in_specs=[seq] * len(args) + [par] * len(params),
        out_specs=seq,
        out_shape=jax.ShapeDtypeStruct((T, D), BF16),
        scratch_shapes=scratch,
        compiler_params=_cparams(3, 2 * ts * width * 4 * (len(args) + 1) + scratch_bytes
                                 + 16 * 1024 * 1024),
        name="wkv7_chunked",
    )(*args, *params)


def _pad_to(x, n, axis):
    pad = n - x.shape[axis]
    if pad == 0:
        return x
    widths = [(0, 0)] * x.ndim
    widths[axis] = (0, pad)
    return jnp.pad(x, widths)


def _lora(x, w_down, w_up, *, bias=None, act_mid=None, act_out=None, out_dtype=F32, name="lora"):
    inner = -(-w_down.shape[1] // LANE) * LANE
    wd = _pad_to(w_down, inner, 1).astype(BF16)
    wu = _pad_to(w_up, inner, 0).astype(BF16)
    mid = _matmul([(x, wd)], out_dtype=BF16, act=act_mid, name=name + "_down")
    return _matmul([(mid, wu)], out_dtype=out_dtype, bias=bias, act=act_out, name=name + "_up")


def _stacked(w):
    if isinstance(w, tuple):
        return w
    return w.astype(BF16)[None], 0


def _hybrid_layer(x, B, S, norm, w_in, q_norm, k_norm, lam_re, lam_im, log_dt, b_re, b_im, c_re, c_im,
                  d_skip, glu_w, glu_b, w_out):
    (w_in, l_in), (glu_w, l_glu), (w_out, l_out) = _stacked(w_in), _stacked(glu_w), _stacked(w_out)
    sw = d_skip.shape[0]
    aw = w_out.shape[1] - sw
    assert aw == sw
    n_heads = aw // ATTN_HEAD_DIM
    h = _rmsnorm(x, norm)
    qkvu = _matmul([(h, w_in, l_in, 0)], out_dtype=BF16, name="hybrid_in_proj")
    y_attn = _attention(qkvu, q_norm, k_norm, B, S, n_heads)
    ops = _s5_chunk_operators(lam_re, lam_im, log_dt, b_re, b_im, c_re, c_im)
    z = _s5(qkvu[:, 3 * aw:], ops, d_skip, B, S)
    y_ssm = _matmul([(z, glu_w, l_glu, 0)], out_dtype=BF16, bias=glu_b, act="sigmoid", mul=z,
                    name="ssm_glu")
    return _matmul([(y_attn, w_out, l_out, 0), (y_ssm, w_out, l_out, 1)], out_dtype=F32, residual=x,
                   name="hybrid_out_proj")


def _rwkv_layer(x, B, S, norm, mu, w_r, w_k, w_v, w0, w1, w2, a0, a1, a2, g1, g2, k_k, k_a, r_k,
                ln_w, ln_b, w_o, v_first, v_mix):
    l_r, l_w, l_k, l_v, l_a, l_g = _rwkv_lerp(x, norm, mu, S)
    r = _matmul([(l_r, *_stacked(w_r), 0)], out_dtype=F32, name="rwkv_r")
    k = _matmul([(l_k, *_stacked(w_k), 0)], out_dtype=F32, name="rwkv_k")
    v = _matmul([(l_v, *_stacked(w_v), 0)], out_dtype=F32, name="rwkv_v")
    wpre = _lora(l_w, w1, w2, bias=w0, act_mid="tanh", name="rwkv_decay")
    a = _lora(l_a, a1, a2, bias=a0, act_out="sigmoid", name="rwkv_iclr")
    g = _lora(l_g, g1, g2, act_mid="sigmoid", out_dtype=BF16, name="rwkv_gate")
    if v_mix is None:
        v_gate, vf = None, None
        v_first = v
    else:
        v0, v1, v2 = v_mix
        v_gate = _lora(l_v, v1, v2, bias=v0, act_out="sigmoid", name="rwkv_vmix")
        vf = v_first
    y = _wkv(r, wpre, k, v, a, g, vf, v_gate, k_k, k_a, r_k.reshape(-1), ln_w, ln_b, B, S)
    return _matmul([(y, *_stacked(w_o), 0)], out_dtype=F32, residual=x, name="rwkv_out_proj"), v_first


def _ffn_layer(x, B, S, norm, w_up, conv_w, conv_b, w_down):
    h = _rmsnorm(x, norm)
    act = _ffn_up_act(h, *_stacked(w_up), conv_w, conv_b, S)
    return _matmul([(act, *_stacked(w_down), 0)], out_dtype=F32, residual=x, name="ffn_down")


def kernel(x, mix_norm, hy_w_in, attn_q_norm, attn_k_norm, ssm_lambda_re, ssm_lambda_im, ssm_log_dt, ssm_b_re, ssm_b_im, ssm_c_re, ssm_c_im, ssm_d, ssm_glu_w, ssm_glu_b, hy_w_out, rwkv_mu, rwkv_w_r, rwkv_w_k, rwkv_w_v, rwkv_w0, rwkv_w1, rwkv_w2, rwkv_a0, rwkv_a1, rwkv_a2, rwkv_g1, rwkv_g2, rwkv_k_k, rwkv_k_a, rwkv_r_k, rwkv_ln_w, rwkv_ln_b, rwkv_w_o, rwkv_v0, rwkv_v1, rwkv_v2, ffn_norm, ffn_w_up, ffn_conv_w, ffn_conv_b, ffn_w_down):
    B, S, D = x.shape
    depth = mix_norm.shape[0]
    x = x.reshape(B * S, D)
    hy_w_in, ssm_glu_w, hy_w_out, rwkv_w_r, rwkv_w_k, rwkv_w_v, rwkv_w_o, ffn_w_up, ffn_w_down = (
        w.astype(BF16) for w in (hy_w_in, ssm_glu_w, hy_w_out, rwkv_w_r, rwkv_w_k, rwkv_w_v,
                                 rwkv_w_o, ffn_w_up, ffn_w_down))
    v_first = None
    for layer in range(depth):
        i = layer // 2
        if layer % 2 == 0:
            x = _hybrid_layer(x, B, S, mix_norm[layer], (hy_w_in, i), attn_q_norm[i], attn_k_norm[i],
                              ssm_lambda_re[i], ssm_lambda_im[i], ssm_log_dt[i], ssm_b_re[i],
                              ssm_b_im[i], ssm_c_re[i], ssm_c_im[i], ssm_d[i], (ssm_glu_w, i),
                              ssm_glu_b[i], (hy_w_out, i))
        else:
            v_mix = None if i == 0 else (rwkv_v0[i - 1], rwkv_v1[i - 1], rwkv_v2[i - 1])
            x, v_first = _rwkv_layer(x, B, S, mix_norm[layer], rwkv_mu[i], (rwkv_w_r, i), (rwkv_w_k, i),
                                     (rwkv_w_v, i), rwkv_w0[i], rwkv_w1[i], rwkv_w2[i], rwkv_a0[i],
                                     rwkv_a1[i], rwkv_a2[i], rwkv_g1[i], rwkv_g2[i], rwkv_k_k[i],
                                     rwkv_k_a[i], rwkv_r_k[i], rwkv_ln_w[i], rwkv_ln_b[i],
                                     (rwkv_w_o, i), v_first, v_mix)
        x = _ffn_layer(x, B, S, ffn_norm[layer], (ffn_w_up, layer), ffn_conv_w[layer],
                       ffn_conv_b[layer], (ffn_w_down, layer))
    return x.reshape(B, S, D)
```

```python
import functools
import math

import numpy as np
import jax
import jax.numpy as jnp
from jax import lax
from jax.experimental import pallas as pl
from jax.experimental.pallas import tpu as pltpu

F32 = jnp.float32
BF16 = jnp.bfloat16
HIGHEST = lax.Precision.HIGHEST

RMS_EPS = 1e-6
ATTN_HEAD_DIM = 128
DILATED_BRANCHES = ((128, 1), (512, 4), (2048, 16))
ROPE_THETA = 10000.0
SSM_GROUP = 16
SSM_STATE = 64
SSM_CHUNK = 16
RWKV_HEAD_DIM = 64
RWKV_GN_EPS = 64e-5
WKV_CHUNK = 64
FFN_CONV = 3

V7X_VMEM_BYTES = 64 * 1024 * 1024
VMEM_BUDGET = 52 * 1024 * 1024
LANE = 128
SUBLANE = 8

NT_DIMS = (((1,), (1,)), ((), ()))
TN_DIMS = (((0,), (0,)), ((), ()))


def _cparams(n_grid, vmem_bytes):
    limit = int(min(max(vmem_bytes, 16 * 1024 * 1024), V7X_VMEM_BYTES - 6 * 1024 * 1024))
    return pltpu.CompilerParams(dimension_semantics=("arbitrary",) * n_grid,
                                vmem_limit_bytes=limit)


def _pick_tile(n, pref, align):
    t = min(pref, n)
    t -= t % align
    while t >= align:
        if n % t == 0:
            return t
        t -= align
    return n


def _sigmoid(x):
    return 1.0 / (1.0 + jnp.exp(-x))


def _gelu(x):
    return 0.5 * x * (1.0 + lax.erf(x * np.float32(math.sqrt(0.5))))


def _rms(x, gain):
    return x * lax.rsqrt(jnp.mean(x * x, axis=-1, keepdims=True) + RMS_EPS) * gain


def _rmsnorm_body(x_ref, g_ref, o_ref):
    o_ref[...] = _rms(x_ref[...], g_ref[...]).astype(o_ref.dtype)


def _rmsnorm(x, gain, out_dtype=BF16):
    T, D = x.shape
    tm = _pick_tile(T, 256, SUBLANE)
    return pl.pallas_call(
        _rmsnorm_body,
        grid=(T // tm,),
        in_specs=[pl.BlockSpec((tm, D), lambda i: (i, 0)),
                  pl.BlockSpec((1, D), lambda i: (0, 0))],
        out_specs=pl.BlockSpec((tm, D), lambda i: (i, 0)),
        out_shape=jax.ShapeDtypeStruct((T, D), out_dtype),
        compiler_params=_cparams(1, 6 * tm * D * 4),
        name="rmsnorm",
    )(x, gain.reshape(1, D))


def _lerp_body(x_ref, xh_ref, g_ref, mu_ref, *o_refs, tm, seq):
    i = pl.program_id(0)
    g = g_ref[...]
    h = _rms(x_ref[...], g)
    hh = _rms(xh_ref[...], g)
    prev = jnp.where((i * tm) % seq == 0, 0.0, hh[SUBLANE - 1:SUBLANE, :])
    hs = pltpu.roll(h, 1, axis=0)
    row = lax.broadcasted_iota(jnp.int32, h.shape, 0)
    xx = jnp.where(row == 0, prev, hs) - h
    for n, o_ref in enumerate(o_refs):
        o_ref[...] = (h + xx * mu_ref[n:n + 1, :]).astype(o_ref.dtype)


def _rwkv_lerp(x, gain, mu, seq):
    T, D = x.shape
    n = mu.shape[0]
    tm = _pick_tile(seq, 128, SUBLANE)
    hb = tm // SUBLANE
    return pl.pallas_call(
        functools.partial(_lerp_body, tm=tm, seq=seq),
        grid=(T // tm,),
        in_specs=[pl.BlockSpec((tm, D), lambda i: (i, 0)),
                  pl.BlockSpec((SUBLANE, D), lambda i: (jnp.maximum(i * hb - 1, 0), 0)),
                  pl.BlockSpec((1, D), lambda i: (0, 0)),
                  pl.BlockSpec((n, D), lambda i: (0, 0))],
        out_specs=[pl.BlockSpec((tm, D), lambda i: (i, 0))] * n,
        out_shape=[jax.ShapeDtypeStruct((T, D), BF16)] * n,
        compiler_params=_cparams(1, tm * D * (2 * 4 + 2 * 2 * n + 4 * 4)),
        name="rwkv_lerp",
    )(x, x, gain.reshape(1, D), mu)


def _mm_body(*refs, n_pairs, has_bias, act, has_mul, has_res):
    acc = None
    idx = 0
    for _ in range(n_pairs):
        d = jnp.dot(refs[idx][...], refs[idx + 1][...], preferred_element_type=F32)
        acc = d if acc is None else acc + d
        idx += 2
    if has_bias:
        acc = acc + refs[idx][...]
        idx += 1
    if act == "tanh":
        acc = jnp.tanh(acc)
    elif act == "sigmoid":
        acc = _sigmoid(acc)
    if has_mul:
        acc = acc * refs[idx][...].astype(F32)
        idx += 1
    if has_res:
        acc = acc + refs[idx][...]
        idx += 1
    o_ref = refs[idx]
    o_ref[...] = acc.astype(o_ref.dtype)


def _mm_tiles(M, N, k_total, out_bytes, extra_bytes):
    for tm_pref, tn_pref in ((1024, 512), (512, 512), (512, 256), (256, 256), (128, 256), (128, 128)):
        tm = _pick_tile(M, tm_pref, 16)
        tn = _pick_tile(N, tn_pref, LANE)
        need = (2 * tm * k_total * 2 + 2 * k_total * tn * 2
                + tm * tn * (2 * out_bytes + 2 * extra_bytes + 8))
        if need <= VMEM_BUDGET:
            return tm, tn, need
    return tm, tn, need


def _matmul(pairs, *, out_dtype, bias=None, act=None, mul=None, residual=None, name="matmul"):
    M = pairs[0][0].shape[0]
    N = pairs[0][1].shape[-1]
    k_total = sum(p[0].shape[1] for p in pairs)
    out_bytes = jnp.dtype(out_dtype).itemsize
    extra = (2 if mul is not None else 0) + (4 if residual is not None else 0)
    tm, tn, need = _mm_tiles(M, N, k_total, out_bytes, extra)
    args, in_specs = [], []
    for pair in pairs:
        a, w = pair[0], pair[1]
        layer, kblock = (pair[2], pair[3]) if len(pair) == 4 else (0, 0)
        if w.ndim == 2:
            w = w[None]
        K = a.shape[1]
        assert a.dtype == BF16 and w.dtype == BF16 and a.shape[0] == M
        assert w.shape[2] == N and w.shape[1] % K == 0
        args += [a, w]
        in_specs += [pl.BlockSpec((tm, K), lambda i, j: (i, 0)),
                     pl.BlockSpec((None, K, tn), lambda i, j, layer=layer, kblock=kblock: (layer, kblock, j))]
    if bias is not None:
        args.append(bias.reshape(1, N).astype(F32))
        in_specs.append(pl.BlockSpec((1, tn), lambda i, j: (0, j)))
    if mul is not None:
        args.append(mul)
        in_specs.append(pl.BlockSpec((tm, tn), lambda i, j: (i, j)))
    if residual is not None:
        args.append(residual)
        in_specs.append(pl.BlockSpec((tm, tn), lambda i, j: (i, j)))
    return pl.pallas_call(
        functools.partial(_mm_body, n_pairs=len(pairs), has_bias=bias is not None, act=act,
                          has_mul=mul is not None, has_res=residual is not None),
        grid=(M // tm, N // tn),
        in_specs=in_specs,
        out_specs=pl.BlockSpec((tm, tn), lambda i, j: (i, j)),
        out_shape=jax.ShapeDtypeStruct((M, N), out_dtype),
        compiler_params=_cparams(2, need + 8 * 1024 * 1024),
        name=name,
    )(*args)


def _attn_bias(S, tq):
    delta = (S - tq) + np.arange(tq)[:, None] - np.arange(S)[None, :]
    mult = np.zeros(delta.shape, np.float64)
    for window, dilation in DILATED_BRANCHES:
        mult += (delta >= 0) & (delta <= window) & (delta % dilation == 0)
    bias = np.where(mult > 0, np.log2(np.maximum(mult, 1.0)), -1e30)
    return jnp.asarray(bias, F32)


def _rope_tables(S):
    half = ATTN_HEAD_DIM // 2
    inv_freq = ROPE_THETA ** (-jnp.arange(half, dtype=F32) / half)
    ang = jnp.arange(S, dtype=jnp.int32).astype(F32)[:, None] * inv_freq[None, :]
    cos, sin = jnp.cos(ang), jnp.sin(ang)
    return jnp.concatenate([cos, cos], axis=-1), jnp.concatenate([-sin, sin], axis=-1)


def _attn_body(q_ref, k_ref, v_ref, qg_ref, kg_ref, cos_ref, sin_ref, bias_ref, o_ref,
               qs_ref, ks_ref, *, S, tq):
    half = ATTN_HEAD_DIM // 2

    def prep(x_ref, g_ref, scale):
        x = _rms(x_ref[...].astype(F32), g_ref[...])
        x = x * cos_ref[...] + pltpu.roll(x, half, axis=1) * sin_ref[...]
        return (x * scale).astype(BF16)

    qs_ref[...] = prep(q_ref, qg_ref, np.float32(ATTN_HEAD_DIM ** -0.5 * math.log2(math.e)))
    ks_ref[...] = prep(k_ref, kg_ref, np.float32(1.0))

    n_blocks = S // tq

    def scores(i):
        return lax.dot_general(qs_ref[i * tq:(i + 1) * tq, :], ks_ref[:(i + 1) * tq, :], NT_DIMS,
                               preferred_element_type=F32)

    s_next = scores(0)
    for i in range(n_blocks):
        s = s_next + bias_ref[:, (n_blocks - 1 - i) * tq:]
        if i + 1 < n_blocks:
            s_next = scores(i + 1)
        p = jnp.exp2(s - jnp.max(s, axis=-1, keepdims=True))
        l = jnp.sum(p, axis=-1, keepdims=True)
        acc = jnp.dot(p.astype(BF16), v_ref[:(i + 1) * tq, :], preferred_element_type=F32)
        o_ref[i * tq:(i + 1) * tq, :] = (acc / l).astype(o_ref.dtype)


def _attention(qkvu, q_gain, k_gain, B, S, n_heads):
    T = B * S
    dh = ATTN_HEAD_DIM
    tq = _pick_tile(S, 256, LANE)
    cos, sin = _rope_tables(S)
    bias = _attn_bias(S, tq)
    head = lambda off: pl.BlockSpec((S, dh), lambda b, h: (b, off + h))
    const2 = lambda shape: pl.BlockSpec(shape, lambda b, h: (0, 0))
    return pl.pallas_call(
        functools.partial(_attn_body, S=S, tq=tq),
        grid=(B, n_heads),
        in_specs=[head(0), head(n_heads), head(2 * n_heads),
                  const2((1, dh)), const2((1, dh)), const2((S, dh)), const2((S, dh)),
                  const2(bias.shape)],
        out_specs=pl.BlockSpec((S, dh), lambda b, h: (b, h)),
        out_shape=jax.ShapeDtypeStruct((T, n_heads * dh), BF16),
        scratch_shapes=[pltpu.VMEM((S, dh), BF16), pltpu.VMEM((S, dh), BF16)],
        compiler_params=_cparams(2, 2 * bias.size * 4 + 24 * S * dh * 4),
        name="dilated_attention",
    )(qkvu, qkvu, qkvu, q_gain.reshape(1, dh), k_gain.reshape(1, dh), cos, sin, bias)


def _s5_chunk_operators(lam_re, lam_im, log_dt, b_re, b_im, c_re, c_im):
    Lc = SSM_CHUNK
    G, N = lam_re.shape
    P = b_re.shape[-1]
    dt = jnp.exp(log_dt)[:, None]
    mag = jnp.exp(lam_re * dt)
    ab_re, ab_im = mag * jnp.cos(lam_im * dt), mag * jnp.sin(lam_im * dt)
    den = lam_re * lam_re + lam_im * lam_im
    nr, ni = ab_re - 1.0, ab_im
    coef_re = (nr * lam_re + ni * lam_im) / den
    coef_im = (ni * lam_re - nr * lam_im) / den
    bb_re = coef_re[..., None] * b_re - coef_im[..., None] * b_im
    bb_im = coef_re[..., None] * b_im + coef_im[..., None] * b_re
    steps = jnp.arange(Lc + 1, dtype=F32)[:, None, None]
    pmag = jnp.exp(lam_re * dt * steps)
    pw_re, pw_im = pmag * jnp.cos(lam_im * dt * steps), pmag * jnp.sin(lam_im * dt * steps)
    ca_re = c_re[None] * pw_re[:Lc, :, None, :] - c_im[None] * pw_im[:Lc, :, None, :]
    ca_im = c_re[None] * pw_im[:Lc, :, None, :] + c_im[None] * pw_re[:Lc, :, None, :]
    lag = (jnp.sum(ca_re[..., None] * bb_re[None, :, None], axis=3)
           - jnp.sum(ca_im[..., None] * bb_im[None, :, None], axis=3))
    rev = Lc - 1 - jnp.arange(Lc)
    bin_re = pw_re[rev][..., None] * bb_re[None] - pw_im[rev][..., None] * bb_im[None]
    bin_im = pw_re[rev][..., None] * bb_im[None] + pw_im[rev][..., None] * bb_re[None]
    nxt_re, nxt_im = pw_re[1:], pw_im[1:]
    co_re = c_re[None] * nxt_re[:, :, None, :] - c_im[None] * nxt_im[:, :, None, :]
    co_im = c_re[None] * nxt_im[:, :, None, :] + c_im[None] * nxt_re[:, :, None, :]

    gb = LANE // P
    nb = G // gb

    def spread(x):
        R, C = x.shape[2:]
        x = x.reshape(Lc, nb, gb, R, C).transpose(1, 0, 2, 3, 4).reshape(nb, Lc, gb * R, C)
        copies = jnp.tile(jnp.eye(C, dtype=BF16), (1, gb))
        y = jnp.dot(x.astype(BF16), copies, preferred_element_type=F32)
        same = (jnp.arange(gb * R)[:, None] // R) == (jnp.arange(gb * C)[None, :] // C)
        return jnp.where(same, y, 0.0).astype(BF16)

    lag8 = spread(lag.transpose(0, 1, 3, 2))
    bin8_re = spread(bin_re.transpose(0, 1, 3, 2)).reshape(nb, Lc * LANE, gb * N)
    bin8_im = spread(bin_im.transpose(0, 1, 3, 2)).reshape(nb, Lc * LANE, gb * N)
    cout8_re = spread(co_re.transpose(0, 1, 3, 2))
    cout8_im = spread((-co_im).transpose(0, 1, 3, 2))
    return (lag8, bin8_re, bin8_im, cout8_re, cout8_im,
            pw_re[Lc].reshape(nb, 1, gb * N), pw_im[Lc].reshape(nb, 1, gb * N))


def _s5_body(u_ref, lag_ref, bre_ref, bim_ref, cre_ref, cim_ref, are_ref, aim_ref, d_ref, z_ref,
             toep_s, cout_re_s, cout_im_s, xin_re, xin_im, xst_re, xst_im, car_re, car_im,
             *, n_chunks, batch):
    Lc = SSM_CHUNK

    @pl.when(pl.program_id(1) == 0)
    def _():
        car_re[...] = jnp.zeros_like(car_re)
        car_im[...] = jnp.zeros_like(car_im)
        toep_s[...] = jnp.zeros_like(toep_s)
        for ti in range(Lc):
            for to in range(ti, Lc):
                toep_s[ti * LANE:(ti + 1) * LANE, to * LANE:(to + 1) * LANE] = lag_ref[0, to - ti]
        for to in range(Lc):
            cout_re_s[:, to * LANE:(to + 1) * LANE] = cre_ref[0, to]
            cout_im_s[:, to * LANE:(to + 1) * LANE] = cim_ref[0, to]

    u = u_ref[0]
    xin_re[...] = jnp.dot(u, bre_ref[0], preferred_element_type=F32)
    xin_im[...] = jnp.dot(u, bim_ref[0], preferred_element_type=F32)
    a_re, a_im = are_ref[0], aim_ref[0]

    def step(c, carry):
        x_re, x_im = carry
        rows = pl.ds(pl.multiple_of(c * batch, batch), batch)
        xst_re[rows, :] = x_re
        xst_im[rows, :] = x_im
        return (a_re * x_re - a_im * x_im + xin_re[rows, :],
                a_re * x_im + a_im * x_re + xin_im[rows, :])

    x_re, x_im = lax.fori_loop(0, n_chunks, step, (car_re[...], car_im[...]))
    car_re[...] = x_re
    car_im[...] = x_im
    y = (jnp.dot(u, toep_s[...], preferred_element_type=F32)
         + jnp.dot(xst_re[...].astype(BF16), cout_re_s[...], preferred_element_type=F32)
         + jnp.dot(xst_im[...].astype(BF16), cout_im_s[...], preferred_element_type=F32))
    y = y + d_ref[0] * u.astype(F32)
    z_ref[0] = _gelu(y).astype(z_ref.dtype)


def _s5(u, ops, d_skip, B, S):
    lag8, bin_re, bin_im, cout_re, cout_im, a_re, a_im = ops
    nb, W, N = bin_re.shape
    Lc = SSM_CHUNK
    nc = S // Lc
    parts = 4 if nc % 4 == 0 else 1
    rows = (nc // parts) * B
    ut = u.reshape(B, nc, Lc, nb, LANE).transpose(3, 1, 0, 2, 4).reshape(nb, nc * B, W)
    d_t = jnp.tile(d_skip.reshape(nb, 1, LANE), (1, Lc, 1)).reshape(nb, 1, W)
    const = lambda shape: pl.BlockSpec((1,) + shape, lambda l, r: (l,) + (0,) * len(shape))
    block = pl.BlockSpec((1, rows, W), lambda l, r: (l, r, 0))
    zt = pl.pallas_call(
        functools.partial(_s5_body, n_chunks=nc // parts, batch=B),
        grid=(nb, parts),
        in_specs=[block, const((Lc, LANE, LANE)), const((W, N)), const((W, N)),
                  const((Lc, N, LANE)), const((Lc, N, LANE)), const((1, N)), const((1, N)),
                  const((1, W))],
        out_specs=block,
        out_shape=jax.ShapeDtypeStruct((nb, nc * B, W), BF16),
        scratch_shapes=([pltpu.VMEM((W, W), BF16), pltpu.VMEM((N, W), BF16), pltpu.VMEM((N, W), BF16)]
                        + [pltpu.VMEM((rows, N), F32)] * 4 + [pltpu.VMEM((B, N), F32)] * 2),
        compiler_params=_cparams(2, (W * W + 10 * W * N) * 2 + rows * W * 24 + rows * N * 32),
        name="s5_chunked_scan",
    )(ut, lag8, bin_re, bin_im, cout_re, cout_im, a_re, a_im, d_t)
    return zt.reshape(nb, nc, B, Lc, LANE).transpose(2, 1, 3, 0, 4).reshape(B * S, nb * LANE)


def _ffn_up_body(h_ref, wg_ref, wv_ref, cw_ref, cb_ref, o_ref, gate_ref, val_ref, tail_ref,
                 *, tm, seq, nj):
    i, j = pl.program_id(0), pl.program_id(1)

    def conv(g, shifted):
        y = cw_ref[FFN_CONV - 1:FFN_CONV, :] * g + cb_ref[...]
        for back in range(1, FFN_CONV):
            y = y + cw_ref[FFN_CONV - 1 - back:FFN_CONV - back, :] * shifted(back)
        return y

    def activate():
        jj = j - 1
        gate, val = gate_ref[...], val_ref[...]
        y = conv(gate, lambda back: pltpu.roll(gate, back, axis=0))
        o_ref[...] = (_gelu(y) * val).astype(o_ref.dtype)
        prev = jnp.where((i * tm) % seq == 0, 0.0, tail_ref[jj])
        top = gate[:SUBLANE, :]
        row = lax.broadcasted_iota(jnp.int32, top.shape, 0)
        y_top = conv(top, lambda back: jnp.where(row < back, pltpu.roll(prev, back, axis=0),
                                                 pltpu.roll(top, back, axis=0)))
        o_ref[:SUBLANE, :] = (_gelu(y_top) * val[:SUBLANE, :]).astype(o_ref.dtype)
        tail_ref[jj] = gate[tm - SUBLANE:, :]

    def multiply():
        n_parts = 8 if tm % 128 == 0 else 1
        part = tm // n_parts
        for rows in (slice(p * part, (p + 1) * part) for p in range(n_parts)):
            h = h_ref[rows, :]
            gate_ref[rows, :] = jnp.dot(h, wg_ref[...], preferred_element_type=F32)
            val_ref[rows, :] = jnp.dot(h, wv_ref[...], preferred_element_type=F32)

    @pl.when(jnp.logical_and(i == 0, j == 0))
    def _():
        tail_ref[...] = jnp.zeros_like(tail_ref)

    @pl.when(j == 0)
    def _():
        multiply()

    @pl.when(jnp.logical_and(j > 0, j < nj))
    def _():
        activate()
        multiply()

    @pl.when(j == nj)
    def _():
        activate()


def _ffn_up_act(h, w_up, layer, conv_w, conv_b, seq):
    T, D = h.shape
    Fd = w_up.shape[2] // 2
    tm = _pick_tile(seq, 1024, 16)
    tn = _pick_tile(Fd, 512, LANE)
    nj = Fd // tn
    cur = lambda j: jnp.minimum(j, nj - 1)
    lag = lambda j: jnp.maximum(j - 1, 0)
    need = 2 * tm * D * 2 + 4 * D * tn * 2 + 2 * tm * tn * 2 + 12 * tm * tn * 4
    return pl.pallas_call(
        functools.partial(_ffn_up_body, tm=tm, seq=seq, nj=nj),
        grid=(T // tm, nj + 1),
        in_specs=[pl.BlockSpec((tm, D), lambda i, j: (i, 0)),
                  pl.BlockSpec((None, D, tn), lambda i, j: (layer, 0, cur(j))),
                  pl.BlockSpec((None, D, tn), lambda i, j: (layer, 0, nj + cur(j))),
                  pl.BlockSpec((FFN_CONV, tn), lambda i, j: (0, lag(j))),
                  pl.BlockSpec((1, tn), lambda i, j: (0, lag(j)))],
        out_specs=pl.BlockSpec((tm, tn), lambda i, j: (i, lag(j))),
        out_shape=jax.ShapeDtypeStruct((T, Fd), BF16),
        scratch_shapes=[pltpu.VMEM((tm, tn), F32), pltpu.VMEM((tm, tn), F32),
                        pltpu.VMEM((nj, SUBLANE, tn), F32)],
        compiler_params=_cparams(2, need),
        name="ffn_up_convglu",
    )(h, w_up, w_up, conv_w, conv_b.reshape(1, Fd))


def _wkv_body(*refs, n_chunks, n_pairs, has_vmix):
    C = WKV_CHUNK
    nd = RWKV_HEAD_DIM
    lanes = 2 * nd
    n_seq = 8 if has_vmix else 6
    r_ref, w_ref, k_ref, v_ref, a_ref, g_ref = refs[:6]
    vf_ref, vg_ref = refs[6:8] if has_vmix else (None, None)
    kk_ref, ka_ref, rk_ref, lnw_ref, lnb_ref, o_ref = refs[n_seq:n_seq + 6]
    (st_ref, inv_s, t_s, arbk_s, left_s, dk_s, v_s, v2_s, bonus_s, dec_s, u_s, ls_s) = refs[n_seq + 6:]
    unroll = 2 if (n_chunks % 2 == 0 and n_pairs <= 8) else 1

    @pl.when(pl.program_id(2) == 0)
    def _():
        st_ref[...] = jnp.zeros_like(st_ref)

    lane_r = lax.broadcasted_iota(jnp.int32, (lanes, lanes), 0)
    lane_c = lax.broadcasted_iota(jnp.int32, (lanes, lanes), 1)
    same_head = (lane_r // nd) == (lane_c // nd)
    head_ones = same_head.astype(BF16)
    pos_r = lax.broadcasted_iota(jnp.int32, (C, lanes), 0)
    pos_c = lax.broadcasted_iota(jnp.int32, (C, lanes), 1) % C
    strict = pos_r > pos_c
    incl = pos_r >= pos_c
    eye = (pos_r == pos_c).astype(F32)
    tri = (lax.broadcasted_iota(jnp.int32, (C, C), 0)
           >= lax.broadcasted_iota(jnp.int32, (C, C), 1)).astype(BF16)
    lane1 = lax.broadcasted_iota(jnp.int32, (1, lanes), 1)
    m0 = (lane1 < nd).astype(F32)
    m1 = 1.0 - m0

    def split(x):
        hi = x.astype(BF16)
        return hi, (x - hi.astype(F32)).astype(BF16)

    def head_sum(x):
        n = x.shape[0]
        z = jnp.dot(jnp.concatenate(split(x), axis=0), head_ones, preferred_element_type=F32)
        return z[:n] + z[n:]

    def stack(x):
        return jnp.concatenate([x * m0, x * m1], axis=0).astype(BF16)

    def bdot(a, b, dims=None):
        a, b = a.astype(BF16), b.astype(BF16)
        if dims is None:
            return jnp.dot(a, b, preferred_element_type=F32)
        return lax.dot_general(a, b, dims, preferred_element_type=F32)

    def problems(i):
        out = []
        for cc in range(unroll):
            c = i * unroll + cc
            for p in range(n_pairs):
                out.append((c * n_pairs + p, pl.ds(pl.multiple_of(c * C, C), C),
                            slice(p * lanes, (p + 1) * lanes), p))
        return out

    def pass_a(i, _):
        prob = problems(i)
        ld = lambda ref: [ref[rows, cols] for _, rows, cols, _ in prob]
        par = lambda ref: [ref[:, cols] for _, _, cols, _ in prob]
        n = range(len(prob))
        r, k, v, a, wpre = ld(r_ref), ld(k_ref), ld(v_ref), ld(a_ref), ld(w_ref)
        k_k, k_a, r_k = par(kk_ref), par(ka_ref), par(rk_ref)
        if has_vmix:
            vf, vg = ld(vf_ref), ld(vg_ref)
            v = [v[j] + (vf[j] - v[j]) * vg[j] for j in n]
        dlog = [-jnp.exp(-(jnp.maximum(-w, 0.0) + jnp.log(1.0 + jnp.exp(-jnp.abs(w)))) - 0.5)
                for w in wpre]
        kk = [k[j] * k_k[j] for j in n]
        k2 = [k[j] * (1.0 + (a[j] - 1.0) * k_a[j]) for j in n]
        sums = [head_sum(jnp.concatenate([kk[j] * kk[j], r[j] * k2[j] * r_k[j]], axis=0)) for j in n]
        z = [jnp.dot(tri, jnp.concatenate(split(dlog[j]), axis=1), preferred_element_type=F32)
             for j in n]
        lg = [zz[:, :lanes] + zz[:, lanes:] for zz in z]
        kk = [kk[j] / jnp.maximum(jnp.sqrt(sums[j][:C]), 1e-12) for j in n]
        bv = [kk[j] * a[j] for j in n]
        e_neg = [jnp.exp(-l) for l in lg]
        e_end = [jnp.exp(l[C - 1:C, :] - l) for l in lg]
        left = [jnp.concatenate([-kk[j] * jnp.exp(lg[j] - dlog[j]), r[j] * jnp.exp(lg[j])],
                                axis=0).astype(BF16) for j in n]
        right = [jnp.concatenate([stack(bv[j] * e_neg[j]), stack(k2[j] * e_neg[j])], axis=0)
                 for j in n]
        aa = [bdot(left[j], right[j], NT_DIMS) for j in n]
        a_ab = [jnp.where(strict, m[:C, :lanes], 0.0) for m in aa]
        v2 = [stack(x) for x in v]
        t = [bdot(jnp.where(strict, aa[j][:C, lanes:], 0.0), v2[j]) for j in n]
        power = [bdot(m, stack(m)) for m in a_ab]
        inv = [eye + m for m in a_ab]
        for _ in range(2, int(math.log2(C))):
            z = [bdot(jnp.concatenate([inv[j], power[j]], axis=0), stack(power[j])) for j in n]
            inv = [inv[j] + z[j][:C] for j in n]
            power = [zz[C:] for zz in z]
        inv = [inv[j] + bdot(inv[j], stack(power[j])) for j in n]
        for j, (slot, rows, cols, _) in enumerate(prob):
            inv_s[slot] = inv[j].astype(BF16)
            t_s[slot] = t[j]
            arbk_s[slot] = jnp.where(jnp.concatenate([incl, incl], axis=1), aa[j][C:], 0.0).astype(BF16)
            left_s[slot] = left[j]
            dk_s[slot] = jnp.concatenate([bv[j] * e_end[j], k2[j] * e_end[j]], axis=0).astype(BF16)
            v_s[slot] = v[j].astype(BF16)
            v2_s[slot] = v2[j]
            bonus_s[slot] = sums[j][C:] * v[j]
            dec_s[slot] = jnp.exp(lg[j][C - 1:C, :])
        return 0

    def pass_b(c, _):
        n = range(n_pairs)
        slot = [c * n_pairs + p for p in n]
        state = [st_ref[p] for p in n]
        ls = [bdot(left_s[slot[p]], state[p], NT_DIMS) for p in n]
        u = [bdot(inv_s[slot[p]], stack(ls[p][:C] + t_s[slot[p]])) for p in n]
        u = [x.astype(BF16) for x in u]
        upd = [lax.dot_general(jnp.concatenate([u[p], v_s[slot[p]]], axis=0), dk_s[slot[p]], TN_DIMS,
                               preferred_element_type=F32) for p in n]
        for p in n:
            st_ref[p] = state[p] * dec_s[slot[p]] + jnp.where(same_head, upd[p], 0.0)
            u_s[slot[p]] = u[p]
            ls_s[slot[p]] = ls[p][C:]
        return 0

    def pass_c(i, _):
        prob = problems(i)
        n = range(len(prob))
        uv = [jnp.concatenate([stack(u_s[slot].astype(F32)), v2_s[slot]], axis=0)
              for slot, _, _, _ in prob]
        y = [ls_s[prob[j][0]] + jnp.dot(arbk_s[prob[j][0]], uv[j], preferred_element_type=F32)
             for j in n]
        mean = [head_sum(x) * np.float32(1.0 / nd) for x in y]
        d = [y[j] - mean[j] for j in n]
        var = [head_sum(x * x) * np.float32(1.0 / nd) for x in d]
        for j, (slot, rows, cols, _) in enumerate(prob):
            yn = d[j] * lax.rsqrt(var[j] + RWKV_GN_EPS) * lnw_ref[:, cols] + lnb_ref[:, cols]
            o_ref[rows, cols] = ((yn + bonus_s[slot]) * g_ref[rows, cols].astype(F32)).astype(o_ref.dtype)
        return 0

    lax.fori_loop(0, n_chunks // unroll, pass_a, 0)
    lax.fori_loop(0, n_chunks, pass_b, 0)
    lax.fori_loop(0, n_chunks // unroll, pass_c, 0)


def _wkv(r, wpre, k, v, a, g, v_first, v_gate, k_k, k_a, r_k, ln_w, ln_b, B, S):
    T, D = r.shape
    lanes = 2 * RWKV_HEAD_DIM
    has_vmix = v_first is not None
    n_pairs = next(n for n in (16, 8, 4, 1) if D % (n * lanes) == 0)
    width = n_pairs * lanes
    ts = _pick_tile(S, 4096 // n_pairs, WKV_CHUNK)
    nt = S // ts
    seq = pl.BlockSpec((ts, width), lambda b, h, t: (b * nt + t, h))
    par = pl.BlockSpec((1, width), lambda b, h, t: (0, h))
    args = [r, wpre, k, v, a, g] + ([v_first, v_gate] if has_vmix else [])
    params = [p.reshape(1, D) for p in (k_k, k_a, r_k, ln_w, ln_b)]
    C = WKV_CHUNK
    n_chunks = ts // C
    slots = n_chunks * n_pairs
    scratch = [pltpu.VMEM((n_pairs, lanes, lanes), F32),
               pltpu.VMEM((slots, C, lanes), BF16),
               pltpu.VMEM((slots, C, lanes), F32),
               pltpu.VMEM((slots, C, 2 * lanes), BF16),
               pltpu.VMEM((slots, 2 * C, lanes), BF16),
               pltpu.VMEM((slots, 2 * C, lanes), BF16),
               pltpu.VMEM((slots, C, lanes), BF16),
               pltpu.VMEM((slots, 2 * C, lanes), BF16),
               pltpu.VMEM((slots, C, lanes), F32),
               pltpu.VMEM((slots, 1, lanes), F32),
               pltpu.VMEM((slots, C, lanes), BF16),
               pltpu.VMEM((slots, C, lanes), F32)]
    scratch_bytes = slots * C * lanes * 30
    return pl.pallas_call(
        functools.partial(_wkv_body, n_chunks=n_chunks, n_pairs=n_pairs, has_vmix=has_vmix),
        grid=(B, D // width, nt),
        in_specs=[seq] * len(args) + [par] * len(params),
        out_specs=seq,
        out_shape=jax.ShapeDtypeStruct((T, D), BF16),
        scratch_shapes=scratch,
        compiler_params=_cparams(3, 2 * ts * width * 4 * (len(args) + 1) + scratch_bytes
                                 + 16 * 1024 * 1024),
        name="wkv7_chunked",
    )(*args, *params)


def _pad_to(x, n, axis):
    pad = n - x.shape[axis]
    if pad == 0:
        return x
    widths = [(0, 0)] * x.ndim
    widths[axis] = (0, pad)
    return jnp.pad(x, widths)


def _lora(x, w_down, w_up, *, bias=None, act_mid=None, act_out=None, out_dtype=F32, name="lora"):
    inner = -(-w_down.shape[1] // LANE) * LANE
    wd = _pad_to(w_down, inner, 1).astype(BF16)
    wu = _pad_to(w_up, inner, 0).astype(BF16)
    mid = _matmul([(x, wd)], out_dtype=BF16, act=act_mid, name=name + "_down")
    return _matmul([(mid, wu)], out_dtype=out_dtype, bias=bias, act=act_out, name=name + "_up")


def _stacked(w):
    if isinstance(w, tuple):
        return w
    return w.astype(BF16)[None], 0


def _hybrid_layer(x, B, S, norm, w_in, q_norm, k_norm, lam_re, lam_im, log_dt, b_re, b_im, c_re, c_im,
                  d_skip, glu_w, glu_b, w_out):
    (w_in, l_in), (glu_w, l_glu), (w_out, l_out) = _stacked(w_in), _stacked(glu_w), _stacked(w_out)
    sw = d_skip.shape[0]
    aw = w_out.shape[1] - sw
    assert aw == sw
    n_heads = aw // ATTN_HEAD_DIM
    h = _rmsnorm(x, norm)
    qkvu = _matmul([(h, w_in, l_in, 0)], out_dtype=BF16, name="hybrid_in_proj")
    y_attn = _attention(qkvu, q_norm, k_norm, B, S, n_heads)
    ops = _s5_chunk_operators(lam_re, lam_im, log_dt, b_re, b_im, c_re, c_im)
    z = _s5(qkvu[:, 3 * aw:], ops, d_skip, B, S)
    y_ssm = _matmul([(z, glu_w, l_glu, 0)], out_dtype=BF16, bias=glu_b, act="sigmoid", mul=z,
                    name="ssm_glu")
    return _matmul([(y_attn, w_out, l_out, 0), (y_ssm, w_out, l_out, 1)], out_dtype=F32, residual=x,
                   name="hybrid_out_proj")


def _rwkv_layer(x, B, S, norm, mu, w_r, w_k, w_v, w0, w1, w2, a0, a1, a2, g1, g2, k_k, k_a, r_k,
                ln_w, ln_b, w_o, v_first, v_mix):
    l_r, l_w, l_k, l_v, l_a, l_g = _rwkv_lerp(x, norm, mu, S)
    r = _matmul([(l_r, *_stacked(w_r), 0)], out_dtype=F32, name="rwkv_r")
    k = _matmul([(l_k, *_stacked(w_k), 0)], out_dtype=F32, name="rwkv_k")
    v = _matmul([(l_v, *_stacked(w_v), 0)], out_dtype=F32, name="rwkv_v")
    wpre = _lora(l_w, w1, w2, bias=w0, act_mid="tanh", name="rwkv_decay")
    a = _lora(l_a, a1, a2, bias=a0, act_out="sigmoid", name="rwkv_iclr")
    g = _lora(l_g, g1, g2, act_mid="sigmoid", out_dtype=BF16, name="rwkv_gate")
    if v_mix is None:
        v_gate, vf = None, None
        v_first = v
    else:
        v0, v1, v2 = v_mix
        v_gate = _lora(l_v, v1, v2, bias=v0, act_out="sigmoid", name="rwkv_vmix")
        vf = v_first
    y = _wkv(r, wpre, k, v, a, g, vf, v_gate, k_k, k_a, r_k.reshape(-1), ln_w, ln_b, B, S)
    return _matmul([(y, *_stacked(w_o), 0)], out_dtype=F32, residual=x, name="rwkv_out_proj"), v_first


def _ffn_layer(x, B, S, norm, w_up, conv_w, conv_b, w_down):
    h = _rmsnorm(x, norm)
    act = _ffn_up_act(h, *_stacked(w_up), conv_w, conv_b, S)
    return _matmul([(act, *_stacked(w_down), 0)], out_dtype=F32, residual=x, name="ffn_down")


def kernel(x, mix_norm, hy_w_in, attn_q_norm, attn_k_norm, ssm_lambda_re, ssm_lambda_im, ssm_log_dt, ssm_b_re, ssm_b_im, ssm_c_re, ssm_c_im, ssm_d, ssm_glu_w, ssm_glu_b, hy_w_out, rwkv_mu, rwkv_w_r, rwkv_w_k, rwkv_w_v, rwkv_w0, rwkv_w1, rwkv_w2, rwkv_a0, rwkv_a1, rwkv_a2, rwkv_g1, rwkv_g2, rwkv_k_k, rwkv_k_a, rwkv_r_k, rwkv_ln_w, rwkv_ln_b, rwkv_w_o, rwkv_v0, rwkv_v1, rwkv_v2, ffn_norm, ffn_w_up, ffn_conv_w, ffn_conv_b, ffn_w_down):
    B, S, D = x.shape
    depth = mix_norm.shape[0]
    x = x.reshape(B * S, D)
    hy_w_in, ssm_glu_w, hy_w_out, rwkv_w_r, rwkv_w_k, rwkv_w_v, rwkv_w_o, ffn_w_up, ffn_w_down = (
        w.astype(BF16) for w in (hy_w_in, ssm_glu_w, hy_w_out, rwkv_w_r, rwkv_w_k, rwkv_w_v,
                                 rwkv_w_o, ffn_w_up, ffn_w_down))
    v_first = None
    for layer in range(depth):
        i = layer // 2
        if layer % 2 == 0:
            x = _hybrid_layer(x, B, S, mix_norm[layer], (hy_w_in, i), attn_q_norm[i], attn_k_norm[i],
                              ssm_lambda_re[i], ssm_lambda_im[i], ssm_log_dt[i], ssm_b_re[i],
                              ssm_b_im[i], ssm_c_re[i], ssm_c_im[i], ssm_d[i], (ssm_glu_w, i),
                              ssm_glu_b[i], (hy_w_out, i))
        else:
            v_mix = None if i == 0 else (rwkv_v0[i - 1], rwkv_v1[i - 1], rwkv_v2[i - 1])
            x, v_first = _rwkv_layer(x, B, S, mix_norm[layer], rwkv_mu[i], (rwkv_w_r, i), (rwkv_w_k, i),
                                     (rwkv_w_v, i), rwkv_w0[i], rwkv_w1[i], rwkv_w2[i], rwkv_a0[i],
                                     rwkv_a1[i], rwkv_a2[i], rwkv_g1[i], rwkv_g2[i], rwkv_k_k[i],
                                     rwkv_k_a[i], rwkv_r_k[i], rwkv_ln_w[i], rwkv_ln_b[i],
                                     (rwkv_w_o, i), v_first, v_mix)
        x = _ffn_layer(x, B, S, ffn_norm[layer], (ffn_w_up, layer), ffn_conv_w[layer],
                       ffn_conv_b[layer], (ffn_w_down, layer))
    return x.reshape(B, S, D)
```

```python
import functools
import math

import numpy as np
import jax
import jax.numpy as jnp
from jax import lax
from jax.experimental import pallas as pl
from jax.experimental.pallas import tpu as pltpu

F32 = jnp.float32
BF16 = jnp.bfloat16
HIGHEST = lax.Precision.HIGHEST

RMS_EPS = 1e-6
ATTN_HEAD_DIM = 128
DILATED_BRANCHES = ((128, 1), (512, 4), (2048, 16))
ROPE_THETA = 10000.0
SSM_GROUP = 16
SSM_STATE = 64
SSM_CHUNK = 16
RWKV_HEAD_DIM = 64
RWKV_GN_EPS = 64e-5
WKV_CHUNK = 64
FFN_CONV = 3

V7X_VMEM_BYTES = 64 * 1024 * 1024
VMEM_BUDGET = 52 * 1024 * 1024
LANE = 128
SUBLANE = 8

NT_DIMS = (((1,), (1,)), ((), ()))
TN_DIMS = (((0,), (0,)), ((), ()))


def _cparams(n_grid, vmem_bytes):
    limit = int(min(max(vmem_bytes, 16 * 1024 * 1024), V7X_VMEM_BYTES - 6 * 1024 * 1024))
    return pltpu.CompilerParams(dimension_semantics=("arbitrary",) * n_grid,
                                vmem_limit_bytes=limit)


def _pick_tile(n, pref, align):
    t = min(pref, n)
    t -= t % align
    while t >= align:
        if n % t == 0:
            return t
        t -= align
    return n


def _sigmoid(x):
    return 1.0 / (1.0 + jnp.exp(-x))


def _gelu(x):
    return 0.5 * x * (1.0 + lax.erf(x * np.float32(math.sqrt(0.5))))


def _rms(x, gain):
    return x * lax.rsqrt(jnp.mean(x * x, axis=-1, keepdims=True) + RMS_EPS) * gain


def _rmsnorm_body(x_ref, g_ref, o_ref):
    o_ref[...] = _rms(x_ref[...], g_ref[...]).astype(o_ref.dtype)


def _rmsnorm(x, gain, out_dtype=BF16):
    T, D = x.shape
    tm = _pick_tile(T, 256, SUBLANE)
    return pl.pallas_call(
        _rmsnorm_body,
        grid=(T // tm,),
        in_specs=[pl.BlockSpec((tm, D), lambda i: (i, 0)),
                  pl.BlockSpec((1, D), lambda i: (0, 0))],
        out_specs=pl.BlockSpec((tm, D), lambda i: (i, 0)),
        out_shape=jax.ShapeDtypeStruct((T, D), out_dtype),
        compiler_params=_cparams(1, 6 * tm * D * 4),
        name="rmsnorm",
    )(x, gain.reshape(1, D))


def _lerp_body(x_ref, xh_ref, g_ref, mu_ref, *o_refs, tm, seq):
    i = pl.program_id(0)
    g = g_ref[...]
    h = _rms(x_ref[...], g)
    hh = _rms(xh_ref[...], g)
    prev = jnp.where((i * tm) % seq == 0, 0.0, hh[SUBLANE - 1:SUBLANE, :])
    hs = pltpu.roll(h, 1, axis=0)
    row = lax.broadcasted_iota(jnp.int32, h.shape, 0)
    xx = jnp.where(row == 0, prev, hs) - h
    for n, o_ref in enumerate(o_refs):
        o_ref[...] = (h + xx * mu_ref[n:n + 1, :]).astype(o_ref.dtype)


def _rwkv_lerp(x, gain, mu, seq):
    T, D = x.shape
    n = mu.shape[0]
    tm = _pick_tile(seq, 128, SUBLANE)
    hb = tm // SUBLANE
    return pl.pallas_call(
        functools.partial(_lerp_body, tm=tm, seq=seq),
        grid=(T // tm,),
        in_specs=[pl.BlockSpec((tm, D), lambda i: (i, 0)),
                  pl.BlockSpec((SUBLANE, D), lambda i: (jnp.maximum(i * hb - 1, 0), 0)),
                  pl.BlockSpec((1, D), lambda i: (0, 0)),
                  pl.BlockSpec((n, D), lambda i: (0, 0))],
        out_specs=[pl.BlockSpec((tm, D), lambda i: (i, 0))] * n,
        out_shape=[jax.ShapeDtypeStruct((T, D), BF16)] * n,
        compiler_params=_cparams(1, tm * D * (2 * 4 + 2 * 2 * n + 4 * 4)),
        name="rwkv_lerp",
    )(x, x, gain.reshape(1, D), mu)


def _mm_body(*refs, n_pairs, has_bias, act, has_mul, has_res):
    acc = None
    idx = 0
    for _ in range(n_pairs):
        d = jnp.dot(refs[idx][...], refs[idx + 1][...], preferred_element_type=F32)
        acc = d if acc is None else acc + d
        idx += 2
    if has_bias:
        acc = acc + refs[idx][...]
        idx += 1
    if act == "tanh":
        acc = jnp.tanh(acc)
    elif act == "sigmoid":
        acc = _sigmoid(acc)
    if has_mul:
        acc = acc * refs[idx][...].astype(F32)
        idx += 1
    if has_res:
        acc = acc + refs[idx][...]
        idx += 1
    o_ref = refs[idx]
    o_ref[...] = acc.astype(o_ref.dtype)


def _mm_tiles(M, N, k_total, out_bytes, extra_bytes):
    for tm_pref, tn_pref in ((1024, 1024), (1024, 512), (512, 512), (512, 256), (256, 256), (128, 256),
                             (128, 128)):
        tm = _pick_tile(M, tm_pref, 16)
        tn = _pick_tile(N, tn_pref, LANE)
        need = (2 * tm * k_total * 2 + 2 * k_total * tn * 2
                + tm * tn * (2 * out_bytes + 2 * extra_bytes + 8))
        if need <= VMEM_BUDGET:
            return tm, tn, need
    return tm, tn, need


def _matmul(pairs, *, out_dtype, bias=None, act=None, mul=None, residual=None, name="matmul"):
    M = pairs[0][0].shape[0]
    N = pairs[0][1].shape[-1]
    k_total = sum(p[0].shape[1] for p in pairs)
    out_bytes = jnp.dtype(out_dtype).itemsize
    extra = (2 if mul is not None else 0) + (4 if residual is not None else 0)
    tm, tn, need = _mm_tiles(M, N, k_total, out_bytes, extra)
    args, in_specs = [], []
    for pair in pairs:
        a, w = pair[0], pair[1]
        layer, kblock = (pair[2], pair[3]) if len(pair) == 4 else (0, 0)
        if w.ndim == 2:
            w = w[None]
        K = a.shape[1]
        assert a.dtype == BF16 and w.dtype == BF16 and a.shape[0] == M
        assert w.shape[2] == N and w.shape[1] % K == 0
        args += [a, w]
        in_specs += [pl.BlockSpec((tm, K), lambda i, j: (i, 0)),
                     pl.BlockSpec((None, K, tn), lambda i, j, layer=layer, kblock=kblock: (layer, kblock, j))]
    if bias is not None:
        args.append(bias.reshape(1, N).astype(F32))
        in_specs.append(pl.BlockSpec((1, tn), lambda i, j: (0, j)))
    if mul is not None:
        args.append(mul)
        in_specs.append(pl.BlockSpec((tm, tn), lambda i, j: (i, j)))
    if residual is not None:
        args.append(residual)
        in_specs.append(pl.BlockSpec((tm, tn), lambda i, j: (i, j)))
    return pl.pallas_call(
        functools.partial(_mm_body, n_pairs=len(pairs), has_bias=bias is not None, act=act,
                          has_mul=mul is not None, has_res=residual is not None),
        grid=(M // tm, N // tn),
        in_specs=in_specs,
        out_specs=pl.BlockSpec((tm, tn), lambda i, j: (i, j)),
        out_shape=jax.ShapeDtypeStruct((M, N), out_dtype),
        compiler_params=_cparams(2, need + 8 * 1024 * 1024),
        name=name,
    )(*args)


def _attn_bias(S, tq):
    delta = (S - tq) + np.arange(tq)[:, None] - np.arange(S)[None, :]
    mult = np.zeros(delta.shape, np.float64)
    for window, dilation in DILATED_BRANCHES:
        mult += (delta >= 0) & (delta <= window) & (delta % dilation == 0)
    bias = np.where(mult > 0, np.log2(np.maximum(mult, 1.0)), -1e30)
    return jnp.asarray(bias, F32)


def _rope_tables(S):
    half = ATTN_HEAD_DIM // 2
    inv_freq = ROPE_THETA ** (-jnp.arange(half, dtype=F32) / half)
    ang = jnp.arange(S, dtype=jnp.int32).astype(F32)[:, None] * inv_freq[None, :]
    cos, sin = jnp.cos(ang), jnp.sin(ang)
    return jnp.concatenate([cos, cos], axis=-1), jnp.concatenate([-sin, sin], axis=-1)


def _attn_body(q_ref, k_ref, v_ref, qg_ref, kg_ref, cos_ref, sin_ref, bias_ref, o_ref,
               qs_ref, ks_ref, *, S, tq):
    half = ATTN_HEAD_DIM // 2

    def prep(x_ref, g_ref, scale):
        x = _rms(x_ref[...].astype(F32), g_ref[...])
        x = x * cos_ref[...] + pltpu.roll(x, half, axis=1) * sin_ref[...]
        return (x * scale).astype(BF16)

    qs_ref[...] = prep(q_ref, qg_ref, np.float32(ATTN_HEAD_DIM ** -0.5 * math.log2(math.e)))
    ks_ref[...] = prep(k_ref, kg_ref, np.float32(1.0))

    n_blocks = S // tq

    def scores(i):
        return lax.dot_general(qs_ref[i * tq:(i + 1) * tq, :], ks_ref[:(i + 1) * tq, :], NT_DIMS,
                               preferred_element_type=F32)

    s_next = scores(0)
    for i in range(n_blocks):
        s = s_next + bias_ref[:, (n_blocks - 1 - i) * tq:]
        if i + 1 < n_blocks:
            s_next = scores(i + 1)
        p = jnp.exp2(s - jnp.max(s, axis=-1, keepdims=True))
        l = jnp.sum(p, axis=-1, keepdims=True)
        acc = jnp.dot(p.astype(BF16), v_ref[:(i + 1) * tq, :], preferred_element_type=F32)
        o_ref[i * tq:(i + 1) * tq, :] = (acc / l).astype(o_ref.dtype)


def _attention(qkvu, q_gain, k_gain, B, S, n_heads):
    T = B * S
    dh = ATTN_HEAD_DIM
    tq = _pick_tile(S, 256, LANE)
    cos, sin = _rope_tables(S)
    bias = _attn_bias(S, tq)
    head = lambda off: pl.BlockSpec((S, dh), lambda b, h: (b, off + h))
    const2 = lambda shape: pl.BlockSpec(shape, lambda b, h: (0, 0))
    return pl.pallas_call(
        functools.partial(_attn_body, S=S, tq=tq),
        grid=(B, n_heads),
        in_specs=[head(0), head(n_heads), head(2 * n_heads),
                  const2((1, dh)), const2((1, dh)), const2((S, dh)), const2((S, dh)),
                  const2(bias.shape)],
        out_specs=pl.BlockSpec((S, dh), lambda b, h: (b, h)),
        out_shape=jax.ShapeDtypeStruct((T, n_heads * dh), BF16),
        scratch_shapes=[pltpu.VMEM((S, dh), BF16), pltpu.VMEM((S, dh), BF16)],
        compiler_params=_cparams(2, 2 * bias.size * 4 + 24 * S * dh * 4),
        name="dilated_attention",
    )(qkvu, qkvu, qkvu, q_gain.reshape(1, dh), k_gain.reshape(1, dh), cos, sin, bias)


def _s5_chunk_operators(lam_re, lam_im, log_dt, b_re, b_im, c_re, c_im):
    Lc = SSM_CHUNK
    G, N = lam_re.shape
    P = b_re.shape[-1]
    dt = jnp.exp(log_dt)[:, None]
    mag = jnp.exp(lam_re * dt)
    ab_re, ab_im = mag * jnp.cos(lam_im * dt), mag * jnp.sin(lam_im * dt)
    den = lam_re * lam_re + lam_im * lam_im
    nr, ni = ab_re - 1.0, ab_im
    coef_re = (nr * lam_re + ni * lam_im) / den
    coef_im = (ni * lam_re - nr * lam_im) / den
    bb_re = coef_re[..., None] * b_re - coef_im[..., None] * b_im
    bb_im = coef_re[..., None] * b_im + coef_im[..., None] * b_re
    steps = jnp.arange(Lc + 1, dtype=F32)[:, None, None]
    pmag = jnp.exp(lam_re * dt * steps)
    pw_re, pw_im = pmag * jnp.cos(lam_im * dt * steps), pmag * jnp.sin(lam_im * dt * steps)
    ca_re = c_re[None] * pw_re[:Lc, :, None, :] - c_im[None] * pw_im[:Lc, :, None, :]
    ca_im = c_re[None] * pw_im[:Lc, :, None, :] + c_im[None] * pw_re[:Lc, :, None, :]
    lag = (jnp.sum(ca_re[..., None] * bb_re[None, :, None], axis=3)
           - jnp.sum(ca_im[..., None] * bb_im[None, :, None], axis=3))
    rev = Lc - 1 - jnp.arange(Lc)
    bin_re = pw_re[rev][..., None] * bb_re[None] - pw_im[rev][..., None] * bb_im[None]
    bin_im = pw_re[rev][..., None] * bb_im[None] + pw_im[rev][..., None] * bb_re[None]
    nxt_re, nxt_im = pw_re[1:], pw_im[1:]
    co_re = c_re[None] * nxt_re[:, :, None, :] - c_im[None] * nxt_im[:, :, None, :]
    co_im = c_re[None] * nxt_im[:, :, None, :] + c_im[None] * nxt_re[:, :, None, :]

    gb = LANE // P
    nb = G // gb

    def spread(x):
        R, C = x.shape[2:]
        x = x.reshape(Lc, nb, gb, R, C).transpose(1, 0, 2, 3, 4).reshape(nb, Lc, gb * R, C)
        copies = jnp.tile(jnp.eye(C, dtype=BF16), (1, gb))
        y = jnp.dot(x.astype(BF16), copies, preferred_element_type=F32)
        same = (jnp.arange(gb * R)[:, None] // R) == (jnp.arange(gb * C)[None, :] // C)
        return jnp.where(same, y, 0.0).astype(BF16)

    lag8 = spread(lag.transpose(0, 1, 3, 2))
    bin8_re = spread(bin_re.transpose(0, 1, 3, 2)).reshape(nb, Lc * LANE, gb * N)
    bin8_im = spread(bin_im.transpose(0, 1, 3, 2)).reshape(nb, Lc * LANE, gb * N)
    cout8_re = spread(co_re.transpose(0, 1, 3, 2))
    cout8_im = spread((-co_im).transpose(0, 1, 3, 2))
    return (lag8, bin8_re, bin8_im, cout8_re, cout8_im,
            pw_re[Lc].reshape(nb, 1, gb * N), pw_im[Lc].reshape(nb, 1, gb * N))


def _s5_body(u_ref, lag_ref, bre_ref, bim_ref, cre_ref, cim_ref, are_ref, aim_ref, d_ref, z_ref,
             toep_s, cout_re_s, cout_im_s, xin_re, xin_im, xst_re, xst_im, car_re, car_im,
             *, n_chunks, batch):
    Lc = SSM_CHUNK

    @pl.when(pl.program_id(1) == 0)
    def _():
        car_re[...] = jnp.zeros_like(car_re)
        car_im[...] = jnp.zeros_like(car_im)
        toep_s[...] = jnp.zeros_like(toep_s)
        for ti in range(Lc):
            for to in range(ti, Lc):
                toep_s[ti * LANE:(ti + 1) * LANE, to * LANE:(to + 1) * LANE] = lag_ref[0, to - ti]
        for to in range(Lc):
            cout_re_s[:, to * LANE:(to + 1) * LANE] = cre_ref[0, to]
            cout_im_s[:, to * LANE:(to + 1) * LANE] = cim_ref[0, to]

    u = u_ref[0]
    xin_re[...] = jnp.dot(u, bre_ref[0], preferred_element_type=F32)
    xin_im[...] = jnp.dot(u, bim_ref[0], preferred_element_type=F32)
    a_re, a_im = are_ref[0], aim_ref[0]

    def step(c, carry):
        x_re, x_im = carry
        rows = pl.ds(pl.multiple_of(c * batch, batch), batch)
        xst_re[rows, :] = x_re
        xst_im[rows, :] = x_im
        return (a_re * x_re - a_im * x_im + xin_re[rows, :],
                a_re * x_im + a_im * x_re + xin_im[rows, :])

    x_re, x_im = lax.fori_loop(0, n_chunks, step, (car_re[...], car_im[...]))
    car_re[...] = x_re
    car_im[...] = x_im
    y = (jnp.dot(u, toep_s[...], preferred_element_type=F32)
         + jnp.dot(xst_re[...].astype(BF16), cout_re_s[...], preferred_element_type=F32)
         + jnp.dot(xst_im[...].astype(BF16), cout_im_s[...], preferred_element_type=F32))
    y = y + d_ref[0] * u.astype(F32)
    z_ref[0] = _gelu(y).astype(z_ref.dtype)


def _s5(u, ops, d_skip, B, S):
    lag8, bin_re, bin_im, cout_re, cout_im, a_re, a_im = ops
    nb, W, N = bin_re.shape
    Lc = SSM_CHUNK
    nc = S // Lc
    parts = 4 if nc % 4 == 0 else 1
    rows = (nc // parts) * B
    ut = u.reshape(B, nc, Lc, nb, LANE).transpose(3, 1, 0, 2, 4).reshape(nb, nc * B, W)
    d_t = jnp.tile(d_skip.reshape(nb, 1, LANE), (1, Lc, 1)).reshape(nb, 1, W)
    const = lambda shape: pl.BlockSpec((1,) + shape, lambda l, r: (l,) + (0,) * len(shape))
    block = pl.BlockSpec((1, rows, W), lambda l, r: (l, r, 0))
    zt = pl.pallas_call(
        functools.partial(_s5_body, n_chunks=nc // parts, batch=B),
        grid=(nb, parts),
        in_specs=[block, const((Lc, LANE, LANE)), const((W, N)), const((W, N)),
                  const((Lc, N, LANE)), const((Lc, N, LANE)), const((1, N)), const((1, N)),
                  const((1, W))],
        out_specs=block,
        out_shape=jax.ShapeDtypeStruct((nb, nc * B, W), BF16),
        scratch_shapes=([pltpu.VMEM((W, W), BF16), pltpu.VMEM((N, W), BF16), pltpu.VMEM((N, W), BF16)]
                        + [pltpu.VMEM((rows, N), F32)] * 4 + [pltpu.VMEM((B, N), F32)] * 2),
        compiler_params=_cparams(2, (W * W + 10 * W * N) * 2 + rows * W * 24 + rows * N * 32),
        name="s5_chunked_scan",
    )(ut, lag8, bin_re, bin_im, cout_re, cout_im, a_re, a_im, d_t)
    return zt.reshape(nb, nc, B, Lc, LANE).transpose(2, 1, 3, 0, 4).reshape(B * S, nb * LANE)


def _ffn_up_body(h_ref, wg_ref, wv_ref, cw_ref, cb_ref, o_ref, gate_ref, val_ref, tail_ref,
                 *, tm, seq, nj):
    i, j = pl.program_id(0), pl.program_id(1)

    def conv(g, shifted):
        y = cw_ref[FFN_CONV - 1:FFN_CONV, :] * g + cb_ref[...]
        for back in range(1, FFN_CONV):
            y = y + cw_ref[FFN_CONV - 1 - back:FFN_CONV - back, :] * shifted(back)
        return y

    def activate():
        jj = j - 1
        gate, val = gate_ref[...], val_ref[...]
        y = conv(gate, lambda back: pltpu.roll(gate, back, axis=0))
        o_ref[...] = (_gelu(y) * val).astype(o_ref.dtype)
        prev = jnp.where((i * tm) % seq == 0, 0.0, tail_ref[jj])
        top = gate[:SUBLANE, :]
        row = lax.broadcasted_iota(jnp.int32, top.shape, 0)
        y_top = conv(top, lambda back: jnp.where(row < back, pltpu.roll(prev, back, axis=0),
                                                 pltpu.roll(top, back, axis=0)))
        o_ref[:SUBLANE, :] = (_gelu(y_top) * val[:SUBLANE, :]).astype(o_ref.dtype)
        tail_ref[jj] = gate[tm - SUBLANE:, :]

    def multiply():
        n_parts = 8 if tm % 128 == 0 else 1
        part = tm // n_parts
        for rows in (slice(p * part, (p + 1) * part) for p in range(n_parts)):
            h = h_ref[rows, :]
            gate_ref[rows, :] = jnp.dot(h, wg_ref[...], preferred_element_type=F32)
            val_ref[rows, :] = jnp.dot(h, wv_ref[...], preferred_element_type=F32)

    @pl.when(jnp.logical_and(i == 0, j == 0))
    def _():
        tail_ref[...] = jnp.zeros_like(tail_ref)

    @pl.when(j == 0)
    def _():
        multiply()

    @pl.when(jnp.logical_and(j > 0, j < nj))
    def _():
        activate()
        multiply()

    @pl.when(j == nj)
    def _():
        activate()


def _ffn_up_act(h, w_up, layer, conv_w, conv_b, seq):
    T, D = h.shape
    Fd = w_up.shape[2] // 2
    tm = _pick_tile(seq, 1024, 16)
    tn = _pick_tile(Fd, 512, LANE)
    nj = Fd // tn
    cur = lambda j: jnp.minimum(j, nj - 1)
    lag = lambda j: jnp.maximum(j - 1, 0)
    need = 2 * tm * D * 2 + 4 * D * tn * 2 + 2 * tm * tn * 2 + 12 * tm * tn * 4
    return pl.pallas_call(
        functools.partial(_ffn_up_body, tm=tm, seq=seq, nj=nj),
        grid=(T // tm, nj + 1),
        in_specs=[pl.BlockSpec((tm, D), lambda i, j: (i, 0)),
                  pl.BlockSpec((None, D, tn), lambda i, j: (layer, 0, cur(j))),
                  pl.BlockSpec((None, D, tn), lambda i, j: (layer, 0, nj + cur(j))),
                  pl.BlockSpec((FFN_CONV, tn), lambda i, j: (0, lag(j))),
                  pl.BlockSpec((1, tn), lambda i, j: (0, lag(j)))],
        out_specs=pl.BlockSpec((tm, tn), lambda i, j: (i, lag(j))),
        out_shape=jax.ShapeDtypeStruct((T, Fd), BF16),
        scratch_shapes=[pltpu.VMEM((tm, tn), F32), pltpu.VMEM((tm, tn), F32),
                        pltpu.VMEM((nj, SUBLANE, tn), F32)],
        compiler_params=_cparams(2, need),
        name="ffn_up_convglu",
    )(h, w_up, w_up, conv_w, conv_b.reshape(1, Fd))


def _wkv_body(*refs, n_chunks, n_pairs, has_vmix):
    C = WKV_CHUNK
    nd = RWKV_HEAD_DIM
    lanes = 2 * nd
    n_seq = 8 if has_vmix else 6
    r_ref, w_ref, k_ref, v_ref, a_ref, g_ref = refs[:6]
    vf_ref, vg_ref = refs[6:8] if has_vmix else (None, None)
    kk_ref, ka_ref, rk_ref, lnw_ref, lnb_ref, o_ref = refs[n_seq:n_seq + 6]
    (st_ref, inv_s, t_s, arbk_s, left_s, dk_s, v_s, v2_s, bonus_s, dec_s, u_s, ls_s) = refs[n_seq + 6:]
    unroll = 2 if (n_chunks % 2 == 0 and n_pairs <= 8) else 1

    @pl.when(pl.program_id(2) == 0)
    def _():
        st_ref[...] = jnp.zeros_like(st_ref)

    lane_r = lax.broadcasted_iota(jnp.int32, (lanes, lanes), 0)
    lane_c = lax.broadcasted_iota(jnp.int32, (lanes, lanes), 1)
    same_head = (lane_r // nd) == (lane_c // nd)
    head_ones = same_head.astype(BF16)
    pos_r = lax.broadcasted_iota(jnp.int32, (C, lanes), 0)
    pos_c = lax.broadcasted_iota(jnp.int32, (C, lanes), 1) % C
    strict = pos_r > pos_c
    incl = pos_r >= pos_c
    eye = (pos_r == pos_c).astype(F32)
    tri = (lax.broadcasted_iota(jnp.int32, (C, C), 0)
           >= lax.broadcasted_iota(jnp.int32, (C, C), 1)).astype(BF16)
    lane1 = lax.broadcasted_iota(jnp.int32, (1, lanes), 1)
    m0 = (lane1 < nd).astype(F32)
    m1 = 1.0 - m0

    def split(x):
        hi = x.astype(BF16)
        return hi, (x - hi.astype(F32)).astype(BF16)

    def head_sum(x):
        n = x.shape[0]
        z = jnp.dot(jnp.concatenate(split(x), axis=0), head_ones, preferred_element_type=F32)
        return z[:n] + z[n:]

    def stack(x):
        return jnp.concatenate([x * m0, x * m1], axis=0).astype(BF16)

    def bdot(a, b, dims=None):
        a, b = a.astype(BF16), b.astype(BF16)
        if dims is None:
            return jnp.dot(a, b, preferred_element_type=F32)
        return lax.dot_general(a, b, dims, preferred_element_type=F32)

    def problems(i):
        out = []
        for cc in range(unroll):
            c = i * unroll + cc
            for p in range(n_pairs):
                out.append((c * n_pairs + p, pl.ds(pl.multiple_of(c * C, C), C),
                            slice(p * lanes, (p + 1) * lanes), p))
        return out

    def pass_a(i, _):
        prob = problems(i)
        ld = lambda ref: [ref[rows, cols] for _, rows, cols, _ in prob]
        par = lambda ref: [ref[:, cols] for _, _, cols, _ in prob]
        n = range(len(prob))
        r, k, v, a, wpre = ld(r_ref), ld(k_ref), ld(v_ref), ld(a_ref), ld(w_ref)
        k_k, k_a, r_k = par(kk_ref), par(ka_ref), par(rk_ref)
        if has_vmix:
            vf, vg = ld(vf_ref), ld(vg_ref)
            v = [v[j] + (vf[j] - v[j]) * vg[j] for j in n]
        dlog = [np.float32(-math.exp(-0.5)) * _sigmoid(w) for w in wpre]
        kk = [k[j] * k_k[j] for j in n]
        k2 = [k[j] * (1.0 + (a[j] - 1.0) * k_a[j]) for j in n]
        sums = [head_sum(jnp.concatenate([kk[j] * kk[j], r[j] * k2[j] * r_k[j]], axis=0)) for j in n]
        z = [jnp.dot(tri, jnp.concatenate(split(dlog[j]), axis=1), preferred_element_type=F32)
             for j in n]
        lg = [zz[:, :lanes] + zz[:, lanes:] for zz in z]
        kk = [kk[j] / jnp.maximum(jnp.sqrt(sums[j][:C]), 1e-12) for j in n]
        bv = [kk[j] * a[j] for j in n]
        e_neg = [jnp.exp(-l) for l in lg]
        e_end = [jnp.exp(l[C - 1:C, :] - l) for l in lg]
        left = [jnp.concatenate([-kk[j] * jnp.exp(lg[j] - dlog[j]), r[j] * jnp.exp(lg[j])],
                                axis=0).astype(BF16) for j in n]
        right = [jnp.concatenate([stack(bv[j] * e_neg[j]), stack(k2[j] * e_neg[j])], axis=0)
                 for j in n]
        aa = [bdot(left[j], right[j], NT_DIMS) for j in n]
        a_ab = [jnp.where(strict, m[:C, :lanes], 0.0) for m in aa]
        v2 = [stack(x) for x in v]
        t = [bdot(jnp.where(strict, aa[j][:C, lanes:], 0.0), v2[j]) for j in n]
        power = [bdot(m, stack(m)) for m in a_ab]
        inv = [eye + m for m in a_ab]
        for _ in range(2, int(math.log2(C))):
            z = [bdot(jnp.concatenate([inv[j], power[j]], axis=0), stack(power[j])) for j in n]
            inv = [inv[j] + z[j][:C] for j in n]
            power = [zz[C:] for zz in z]
        inv = [inv[j] + bdot(inv[j], stack(power[j])) for j in n]
        for j, (slot, rows, cols, _) in enumerate(prob):
            inv_s[slot] = inv[j].astype(BF16)
            t_s[slot] = t[j]
            arbk_s[slot] = jnp.where(jnp.concatenate([incl, incl], axis=1), aa[j][C:], 0.0).astype(BF16)
            left_s[slot] = left[j]
            dk_s[slot] = jnp.concatenate([bv[j] * e_end[j], k2[j] * e_end[j]], axis=0).astype(BF16)
            v_s[slot] = v[j].astype(BF16)
            v2_s[slot] = v2[j]
            bonus_s[slot] = sums[j][C:] * v[j]
            dec_s[slot] = jnp.exp(lg[j][C - 1:C, :])
        return 0

    def pass_b(c, _):
        n = range(n_pairs)
        slot = [c * n_pairs + p for p in n]
        state = [st_ref[p] for p in n]
        ls = [bdot(left_s[slot[p]], state[p], NT_DIMS) for p in n]
        u = [bdot(inv_s[slot[p]], stack(ls[p][:C] + t_s[slot[p]])) for p in n]
        u = [x.astype(BF16) for x in u]
        upd = [lax.dot_general(jnp.concatenate([u[p], v_s[slot[p]]], axis=0), dk_s[slot[p]], TN_DIMS,
                               preferred_element_type=F32) for p in n]
        for p in n:
            st_ref[p] = state[p] * dec_s[slot[p]] + jnp.where(same_head, upd[p], 0.0)
            u_s[slot[p]] = u[p]
            ls_s[slot[p]] = ls[p][C:]
        return 0

    def pass_c(i, _):
        prob = problems(i)
        n = range(len(prob))
        uv = [jnp.concatenate([stack(u_s[slot].astype(F32)), v2_s[slot]], axis=0)
              for slot, _, _, _ in prob]
        y = [ls_s[prob[j][0]] + jnp.dot(arbk_s[prob[j][0]], uv[j], preferred_element_type=F32)
             for j in n]
        mean = [head_sum(x) * np.float32(1.0 / nd) for x in y]
        d = [y[j] - mean[j] for j in n]
        var = [head_sum(x * x) * np.float32(1.0 / nd) for x in d]
        for j, (slot, rows, cols, _) in enumerate(prob):
            yn = d[j] * lax.rsqrt(var[j] + RWKV_GN_EPS) * lnw_ref[:, cols] + lnb_ref[:, cols]
            o_ref[rows, cols] = ((yn + bonus_s[slot]) * g_ref[rows, cols].astype(F32)).astype(o_ref.dtype)
        return 0

    lax.fori_loop(0, n_chunks // unroll, pass_a, 0)
    lax.fori_loop(0, n_chunks, pass_b, 0)
    lax.fori_loop(0, n_chunks // unroll, pass_c, 0)


def _wkv(r, wpre, k, v, a, g, v_first, v_gate, k_k, k_a, r_k, ln_w, ln_b, B, S):
    T, D = r.shape
    lanes = 2 * RWKV_HEAD_DIM
    has_vmix = v_first is not None
    n_pairs = next(n for n in (16, 8, 4, 1) if D % (n * lanes) == 0)
    width = n_pairs * lanes
    ts = _pick_tile(S, 4096 // n_pairs, WKV_CHUNK)
    nt = S // ts
    seq = pl.BlockSpec((ts, width), lambda b, h, t: (b * nt + t, h))
    par = pl.BlockSpec((1, width), lambda b, h, t: (0, h))
    args = [r, wpre, k, v, a, g] + ([v_first, v_gate] if has_vmix else [])
    params = [p.reshape(1, D) for p in (k_k, k_a, r_k, ln_w, ln_b)]
    C = WKV_CHUNK
    n_chunks = ts // C
    slots = n_chunks * n_pairs
    scratch = [pltpu.VMEM((n_pairs, lanes, lanes), F32),
               pltpu.VMEM((slots, C, lanes), BF16),
               pltpu.VMEM((slots, C, lanes), F32),
               pltpu.VMEM((slots, C, 2 * lanes), BF16),
               pltpu.VMEM((slots, 2 * C, lanes), BF16),
               pltpu.VMEM((slots, 2 * C, lanes), BF16),
               pltpu.VMEM((slots, C, lanes), BF16),
               pltpu.VMEM((slots, 2 * C, lanes), BF16),
               pltpu.VMEM((slots, C, lanes), F32),
               pltpu.VMEM((slots, 1, lanes), F32),
               pltpu.VMEM((slots, C, lanes), BF16),
               pltpu.VMEM((slots, C, lanes), F32)]
    scratch_bytes = slots * C * lanes * 30
    return pl.pallas_call(
        functools.partial(_wkv_body, n_chunks=n_chunks, n_pairs=n_pairs, has_vmix=has_vmix),
        grid=(B, D // width, nt),
        in_specs=[seq] * len(args) + [par] * len(params),
        out_specs=seq,
        out_shape=jax.ShapeDtypeStruct((T, D), BF16),
        scratch_shapes=scratch,
        compiler_params=_cparams(3, 2 * ts * width * 4 * (len(args) + 1) + scratch_bytes
                                 + 16 * 1024 * 1024),
        name="wkv7_chunked",
    )(*args, *params)


def _pad_to(x, n, axis):
    pad = n - x.shape[axis]
    if pad == 0:
        return x
    widths = [(0, 0)] * x.ndim
    widths[axis] = (0, pad)
    return jnp.pad(x, widths)


def _lora(x, w_down, w_up, *, bias=None, act_mid=None, act_out=None, out_dtype=F32, name="lora"):
    inner = -(-w_down.shape[1] // LANE) * LANE
    wd = _pad_to(w_down, inner, 1).astype(BF16)
    wu = _pad_to(w_up, inner, 0).astype(BF16)
    mid = _matmul([(x, wd)], out_dtype=BF16, act=act_mid, name=name + "_down")
    return _matmul([(mid, wu)], out_dtype=out_dtype, bias=bias, act=act_out, name=name + "_up")


def _stacked(w):
    if isinstance(w, tuple):
        return w
    return w.astype(BF16)[None], 0


def _hybrid_layer(x, B, S, norm, w_in, q_norm, k_norm, lam_re, lam_im, log_dt, b_re, b_im, c_re, c_im,
                  d_skip, glu_w, glu_b, w_out):
    (w_in, l_in), (glu_w, l_glu), (w_out, l_out) = _stacked(w_in), _stacked(glu_w), _stacked(w_out)
    sw = d_skip.shape[0]
    aw = w_out.shape[1] - sw
    assert aw == sw
    n_heads = aw // ATTN_HEAD_DIM
    h = _rmsnorm(x, norm)
    qkvu = _matmul([(h, w_in, l_in, 0)], out_dtype=BF16, name="hybrid_in_proj")
    y_attn = _attention(qkvu, q_norm, k_norm, B, S, n_heads)
    ops = _s5_chunk_operators(lam_re, lam_im, log_dt, b_re, b_im, c_re, c_im)
    z = _s5(qkvu[:, 3 * aw:], ops, d_skip, B, S)
    y_ssm = _matmul([(z, glu_w, l_glu, 0)], out_dtype=BF16, bias=glu_b, act="sigmoid", mul=z,
                    name="ssm_glu")
    return _matmul([(y_attn, w_out, l_out, 0), (y_ssm, w_out, l_out, 1)], out_dtype=F32, residual=x,
                   name="hybrid_out_proj")


def _rwkv_layer(x, B, S, norm, mu, w_r, w_k, w_v, w0, w1, w2, a0, a1, a2, g1, g2, k_k, k_a, r_k,
                ln_w, ln_b, w_o, v_first, v_mix):
    l_r, l_w, l_k, l_v, l_a, l_g = _rwkv_lerp(x, norm, mu, S)
    r = _matmul([(l_r, *_stacked(w_r), 0)], out_dtype=F32, name="rwkv_r")
    k = _matmul([(l_k, *_stacked(w_k), 0)], out_dtype=F32, name="rwkv_k")
    v = _matmul([(l_v, *_stacked(w_v), 0)], out_dtype=F32, name="rwkv_v")
    wpre = _lora(l_w, w1, w2, bias=w0, act_mid="tanh", name="rwkv_decay")
    a = _lora(l_a, a1, a2, bias=a0, act_out="sigmoid", name="rwkv_iclr")
    g = _lora(l_g, g1, g2, act_mid="sigmoid", out_dtype=BF16, name="rwkv_gate")
    if v_mix is None:
        v_gate, vf = None, None
        v_first = v
    else:
        v0, v1, v2 = v_mix
        v_gate = _lora(l_v, v1, v2, bias=v0, act_out="sigmoid", name="rwkv_vmix")
        vf = v_first
    y = _wkv(r, wpre, k, v, a, g, vf, v_gate, k_k, k_a, r_k.reshape(-1), ln_w, ln_b, B, S)
    return _matmul([(y, *_stacked(w_o), 0)], out_dtype=F32, residual=x, name="rwkv_out_proj"), v_first


def _ffn_layer(x, B, S, norm, w_up, conv_w, conv_b, w_down):
    h = _rmsnorm(x, norm)
    act = _ffn_up_act(h, *_stacked(w_up), conv_w, conv_b, S)
    return _matmul([(act, *_stacked(w_down), 0)], out_dtype=F32, residual=x, name="ffn_down")


def kernel(x, mix_norm, hy_w_in, attn_q_norm, attn_k_norm, ssm_lambda_re, ssm_lambda_im, ssm_log_dt, ssm_b_re, ssm_b_im, ssm_c_re, ssm_c_im, ssm_d, ssm_glu_w, ssm_glu_b, hy_w_out, rwkv_mu, rwkv_w_r, rwkv_w_k, rwkv_w_v, rwkv_w0, rwkv_w1, rwkv_w2, rwkv_a0, rwkv_a1, rwkv_a2, rwkv_g1, rwkv_g2, rwkv_k_k, rwkv_k_a, rwkv_r_k, rwkv_ln_w, rwkv_ln_b, rwkv_w_o, rwkv_v0, rwkv_v1, rwkv_v2, ffn_norm, ffn_w_up, ffn_conv_w, ffn_conv_b, ffn_w_down):
    B, S, D = x.shape
    depth = mix_norm.shape[0]
    x = x.reshape(B * S, D)
    hy_w_in, ssm_glu_w, hy_w_out, rwkv_w_r, rwkv_w_k, rwkv_w_v, rwkv_w_o, ffn_w_up, ffn_w_down = (
        w.astype(BF16) for w in (hy_w_in, ssm_glu_w, hy_w_out, rwkv_w_r, rwkv_w_k, rwkv_w_v,
                                 rwkv_w_o, ffn_w_up, ffn_w_down))
    v_first = None
    for layer in range(depth):
        i = layer // 2
        if layer % 2 == 0:
            x = _hybrid_layer(x, B, S, mix_norm[layer], (hy_w_in, i), attn_q_norm[i], attn_k_norm[i],
                              ssm_lambda_re[i], ssm_lambda_im[i], ssm_log_dt[i], ssm_b_re[i],
                              ssm_b_im[i], ssm_c_re[i], ssm_c_im[i], ssm_d[i], (ssm_glu_w, i),
                              ssm_glu_b[i], (hy_w_out, i))
        else:
            v_mix = None if i == 0 else (rwkv_v0[i - 1], rwkv_v1[i - 1], rwkv_v2[i - 1])
            x, v_first = _rwkv_layer(x, B, S, mix_norm[layer], rwkv_mu[i], (rwkv_w_r, i), (rwkv_w_k, i),
                                     (rwkv_w_v, i), rwkv_w0[i], rwkv_w1[i], rwkv_w2[i], rwkv_a0[i],
                                     rwkv_a1[i], rwkv_a2[i], rwkv_g1[i], rwkv_g2[i], rwkv_k_k[i],
                                     rwkv_k_a[i], rwkv_r_k[i], rwkv_ln_w[i], rwkv_ln_b[i],
                                     (rwkv_w_o, i), v_first, v_mix)
        x = _ffn_layer(x, B, S, ffn_norm[layer], (ffn_w_up, layer), ffn_conv_w[layer],
                       ffn_conv_b[layer], (ffn_w_down, layer))
    return x.reshape(B, S, D)
```

```python
import functools
import math

import numpy as np
import jax
import jax.numpy as jnp
from jax import lax
from jax.experimental import pallas as pl
from jax.experimental.pallas import tpu as pltpu

F32 = jnp.float32
BF16 = jnp.bfloat16
HIGHEST = lax.Precision.HIGHEST

RMS_EPS = 1e-6
ATTN_HEAD_DIM = 128
DILATED_BRANCHES = ((128, 1), (512, 4), (2048, 16))
ROPE_THETA = 10000.0
SSM_GROUP = 16
SSM_STATE = 64
SSM_CHUNK = 16
RWKV_HEAD_DIM = 64
RWKV_GN_EPS = 64e-5
WKV_CHUNK = 64
FFN_CONV = 3

V7X_VMEM_BYTES = 64 * 1024 * 1024
VMEM_BUDGET = 52 * 1024 * 1024
LANE = 128
SUBLANE = 8

NT_DIMS = (((1,), (1,)), ((), ()))
TN_DIMS = (((0,), (0,)), ((), ()))


def _cparams(n_grid, vmem_bytes):
    limit = int(min(max(vmem_bytes, 16 * 1024 * 1024), V7X_VMEM_BYTES - 6 * 1024 * 1024))
    return pltpu.CompilerParams(dimension_semantics=("arbitrary",) * n_grid,
                                vmem_limit_bytes=limit)


def _pick_tile(n, pref, align):
    t = min(pref, n)
    t -= t % align
    while t >= align:
        if n % t == 0:
            return t
        t -= align
    return n


def _sigmoid(x):
    return 1.0 / (1.0 + jnp.exp(-x))


def _gelu(x):
    return 0.5 * x * (1.0 + lax.erf(x * np.float32(math.sqrt(0.5))))


def _rms(x, gain):
    return x * lax.rsqrt(jnp.mean(x * x, axis=-1, keepdims=True) + RMS_EPS) * gain


def _rmsnorm_body(x_ref, g_ref, o_ref):
    o_ref[...] = _rms(x_ref[...], g_ref[...]).astype(o_ref.dtype)


def _rmsnorm(x, gain, out_dtype=BF16):
    T, D = x.shape
    tm = _pick_tile(T, 256, SUBLANE)
    return pl.pallas_call(
        _rmsnorm_body,
        grid=(T // tm,),
        in_specs=[pl.BlockSpec((tm, D), lambda i: (i, 0)),
                  pl.BlockSpec((1, D), lambda i: (0, 0))],
        out_specs=pl.BlockSpec((tm, D), lambda i: (i, 0)),
        out_shape=jax.ShapeDtypeStruct((T, D), out_dtype),
        compiler_params=_cparams(1, 6 * tm * D * 4),
        name="rmsnorm",
    )(x, gain.reshape(1, D))


def _lerp_body(x_ref, xh_ref, g_ref, mu_ref, *o_refs, tm, seq):
    i = pl.program_id(0)
    g = g_ref[...]
    h = _rms(x_ref[...], g)
    hh = _rms(xh_ref[...], g)
    prev = jnp.where((i * tm) % seq == 0, 0.0, hh[SUBLANE - 1:SUBLANE, :])
    hs = pltpu.roll(h, 1, axis=0)
    row = lax.broadcasted_iota(jnp.int32, h.shape, 0)
    xx = jnp.where(row == 0, prev, hs) - h
    for n, o_ref in enumerate(o_refs):
        o_ref[...] = (h + xx * mu_ref[n:n + 1, :]).astype(o_ref.dtype)


def _rwkv_lerp(x, gain, mu, seq):
    T, D = x.shape
    n = mu.shape[0]
    tm = _pick_tile(seq, 128, SUBLANE)
    hb = tm // SUBLANE
    return pl.pallas_call(
        functools.partial(_lerp_body, tm=tm, seq=seq),
        grid=(T // tm,),
        in_specs=[pl.BlockSpec((tm, D), lambda i: (i, 0)),
                  pl.BlockSpec((SUBLANE, D), lambda i: (jnp.maximum(i * hb - 1, 0), 0)),
                  pl.BlockSpec((1, D), lambda i: (0, 0)),
                  pl.BlockSpec((n, D), lambda i: (0, 0))],
        out_specs=[pl.BlockSpec((tm, D), lambda i: (i, 0))] * n,
        out_shape=[jax.ShapeDtypeStruct((T, D), BF16)] * n,
        compiler_params=_cparams(1, tm * D * (2 * 4 + 2 * 2 * n + 4 * 4)),
        name="rwkv_lerp",
    )(x, x, gain.reshape(1, D), mu)


def _mm_body(*refs, n_pairs, has_bias, act, has_mul, has_res):
    acc = None
    idx = 0
    for _ in range(n_pairs):
        d = jnp.dot(refs[idx][...], refs[idx + 1][...], preferred_element_type=F32)
        acc = d if acc is None else acc + d
        idx += 2
    if has_bias:
        acc = acc + refs[idx][...]
        idx += 1
    if act == "tanh":
        acc = jnp.tanh(acc)
    elif act == "sigmoid":
        acc = _sigmoid(acc)
    if has_mul:
        acc = acc * refs[idx][...].astype(F32)
        idx += 1
    if has_res:
        acc = acc + refs[idx][...]
        idx += 1
    o_ref = refs[idx]
    o_ref[...] = acc.astype(o_ref.dtype)


def _mm_tiles(M, N, k_total, out_bytes, extra_bytes):
    for tm_pref, tn_pref in ((1024, 1024), (1024, 512), (512, 512), (512, 256), (256, 256), (128, 256),
                             (128, 128)):
        tm = _pick_tile(M, tm_pref, 16)
        tn = _pick_tile(N, tn_pref, LANE)
        need = (2 * tm * k_total * 2 + 2 * k_total * tn * 2
                + tm * tn * (2 * out_bytes + 2 * extra_bytes + 8))
        if need <= VMEM_BUDGET:
            return tm, tn, need
    return tm, tn, need


def _matmul(pairs, *, out_dtype, bias=None, act=None, mul=None, residual=None, name="matmul"):
    M = pairs[0][0].shape[0]
    N = pairs[0][1].shape[-1]
    k_total = sum(p[0].shape[1] for p in pairs)
    out_bytes = jnp.dtype(out_dtype).itemsize
    extra = (2 if mul is not None else 0) + (4 if residual is not None else 0)
    tm, tn, need = _mm_tiles(M, N, k_total, out_bytes, extra)
    args, in_specs = [], []
    for pair in pairs:
        a, w = pair[0], pair[1]
        layer, kblock = (pair[2], pair[3]) if len(pair) == 4 else (0, 0)
        if w.ndim == 2:
            w = w[None]
        K = a.shape[1]
        assert a.dtype == BF16 and w.dtype == BF16 and a.shape[0] == M
        assert w.shape[2] == N and w.shape[1] % K == 0
        args += [a, w]
        in_specs += [pl.BlockSpec((tm, K), lambda i, j: (i, 0)),
                     pl.BlockSpec((None, K, tn), lambda i, j, layer=layer, kblock=kblock: (layer, kblock, j))]
    if bias is not None:
        args.append(bias.reshape(1, N).astype(F32))
        in_specs.append(pl.BlockSpec((1, tn), lambda i, j: (0, j)))
    if mul is not None:
        args.append(mul)
        in_specs.append(pl.BlockSpec((tm, tn), lambda i, j: (i, j)))
    if residual is not None:
        args.append(residual)
        in_specs.append(pl.BlockSpec((tm, tn), lambda i, j: (i, j)))
    return pl.pallas_call(
        functools.partial(_mm_body, n_pairs=len(pairs), has_bias=bias is not None, act=act,
                          has_mul=mul is not None, has_res=residual is not None),
        grid=(M // tm, N // tn),
        in_specs=in_specs,
        out_specs=pl.BlockSpec((tm, tn), lambda i, j: (i, j)),
        out_shape=jax.ShapeDtypeStruct((M, N), out_dtype),
        compiler_params=_cparams(2, need + 8 * 1024 * 1024),
        name=name,
    )(*args)


def _attn_bias(S, tq):
    delta = (S - tq) + np.arange(tq)[:, None] - np.arange(S)[None, :]
    mult = np.zeros(delta.shape, np.float64)
    for window, dilation in DILATED_BRANCHES:
        mult += (delta >= 0) & (delta <= window) & (delta % dilation == 0)
    bias = np.where(mult > 0, np.log2(np.maximum(mult, 1.0)), -1e30)
    return jnp.asarray(bias, F32)


def _rope_tables(S):
    half = ATTN_HEAD_DIM // 2
    inv_freq = ROPE_THETA ** (-jnp.arange(half, dtype=F32) / half)
    ang = jnp.arange(S, dtype=jnp.int32).astype(F32)[:, None] * inv_freq[None, :]
    cos, sin = jnp.cos(ang), jnp.sin(ang)
    return jnp.concatenate([cos, cos], axis=-1), jnp.concatenate([-sin, sin], axis=-1)


def _attn_body(q_ref, k_ref, v_ref, qg_ref, kg_ref, cos_ref, sin_ref, bias_ref, o_ref,
               qs_ref, ks_ref, *, S, tq, n_h):
    dh = ATTN_HEAD_DIM
    half = dh // 2
    heads = [slice(hh * dh, (hh + 1) * dh) for hh in range(n_h)]

    def prep(x_ref, g_ref, scale, lanes):
        x = _rms(x_ref[:, lanes].astype(F32), g_ref[...])
        x = x * cos_ref[...] + pltpu.roll(x, half, axis=1) * sin_ref[...]
        return (x * scale).astype(BF16)

    for lanes in heads:
        qs_ref[:, lanes] = prep(q_ref, qg_ref, np.float32(dh ** -0.5 * math.log2(math.e)), lanes)
        ks_ref[:, lanes] = prep(k_ref, kg_ref, np.float32(1.0), lanes)

    n_blocks = S // tq

    def scores(i, lanes):
        return lax.dot_general(qs_ref[i * tq:(i + 1) * tq, lanes], ks_ref[:(i + 1) * tq, lanes],
                               NT_DIMS, preferred_element_type=F32)

    s_next = [scores(0, lanes) for lanes in heads]
    for i in range(n_blocks):
        s = [x + bias_ref[:, (n_blocks - 1 - i) * tq:] for x in s_next]
        if i + 1 < n_blocks:
            s_next = [scores(i + 1, lanes) for lanes in heads]
        p = [jnp.exp2(x - jnp.max(x, axis=-1, keepdims=True)) for x in s]
        l = [jnp.sum(x, axis=-1, keepdims=True) for x in p]
        acc = [jnp.dot(p[hh].astype(BF16), v_ref[:(i + 1) * tq, heads[hh]],
                       preferred_element_type=F32) for hh in range(n_h)]
        for hh in range(n_h):
            o_ref[i * tq:(i + 1) * tq, heads[hh]] = (acc[hh] / l[hh]).astype(o_ref.dtype)


def _attention(qkvu, q_gain, k_gain, B, S, n_heads):
    T = B * S
    dh = ATTN_HEAD_DIM
    tq = _pick_tile(S, 256, LANE)
    n_h = 2 if n_heads % 2 == 0 else 1
    cos, sin = _rope_tables(S)
    bias = _attn_bias(S, tq)
    head = lambda off: pl.BlockSpec((S, n_h * dh), lambda b, h: (b, off // n_h + h))
    const2 = lambda shape: pl.BlockSpec(shape, lambda b, h: (0, 0))
    return pl.pallas_call(
        functools.partial(_attn_body, S=S, tq=tq, n_h=n_h),
        grid=(B, n_heads // n_h),
        in_specs=[head(0), head(n_heads), head(2 * n_heads),
                  const2((1, dh)), const2((1, dh)), const2((S, dh)), const2((S, dh)),
                  const2(bias.shape)],
        out_specs=pl.BlockSpec((S, n_h * dh), lambda b, h: (b, h)),
        out_shape=jax.ShapeDtypeStruct((T, n_heads * dh), BF16),
        scratch_shapes=[pltpu.VMEM((S, n_h * dh), BF16), pltpu.VMEM((S, n_h * dh), BF16)],
        compiler_params=_cparams(2, 2 * bias.size * 4 + 24 * n_h * S * dh * 4),
        name="dilated_attention",
    )(qkvu, qkvu, qkvu, q_gain.reshape(1, dh), k_gain.reshape(1, dh), cos, sin, bias)


def _s5_chunk_operators(lam_re, lam_im, log_dt, b_re, b_im, c_re, c_im):
    Lc = SSM_CHUNK
    G, N = lam_re.shape
    P = b_re.shape[-1]
    dt = jnp.exp(log_dt)[:, None]
    mag = jnp.exp(lam_re * dt)
    ab_re, ab_im = mag * jnp.cos(lam_im * dt), mag * jnp.sin(lam_im * dt)
    den = lam_re * lam_re + lam_im * lam_im
    nr, ni = ab_re - 1.0, ab_im
    coef_re = (nr * lam_re + ni * lam_im) / den
    coef_im = (ni * lam_re - nr * lam_im) / den
    bb_re = coef_re[..., None] * b_re - coef_im[..., None] * b_im
    bb_im = coef_re[..., None] * b_im + coef_im[..., None] * b_re
    steps = jnp.arange(Lc + 1, dtype=F32)[:, None, None]
    pmag = jnp.exp(lam_re * dt * steps)
    pw_re, pw_im = pmag * jnp.cos(lam_im * dt * steps), pmag * jnp.sin(lam_im * dt * steps)
    ca_re = c_re[None] * pw_re[:Lc, :, None, :] - c_im[None] * pw_im[:Lc, :, None, :]
    ca_im = c_re[None] * pw_im[:Lc, :, None, :] + c_im[None] * pw_re[:Lc, :, None, :]
    lag = (jnp.sum(ca_re[..., None] * bb_re[None, :, None], axis=3)
           - jnp.sum(ca_im[..., None] * bb_im[None, :, None], axis=3))
    rev = Lc - 1 - jnp.arange(Lc)
    bin_re = pw_re[rev][..., None] * bb_re[None] - pw_im[rev][..., None] * bb_im[None]
    bin_im = pw_re[rev][..., None] * bb_im[None] + pw_im[rev][..., None] * bb_re[None]
    nxt_re, nxt_im = pw_re[1:], pw_im[1:]
    co_re = c_re[None] * nxt_re[:, :, None, :] - c_im[None] * nxt_im[:, :, None, :]
    co_im = c_re[None] * nxt_im[:, :, None, :] + c_im[None] * nxt_re[:, :, None, :]

    gb = LANE // P
    nb = G // gb

    def spread(x):
        R, C = x.shape[2:]
        x = x.reshape(Lc, nb, gb, R, C).transpose(1, 0, 2, 3, 4).reshape(nb, Lc, gb * R, C)
        copies = jnp.tile(jnp.eye(C, dtype=BF16), (1, gb))
        y = jnp.dot(x.astype(BF16), copies, preferred_element_type=F32)
        same = (jnp.arange(gb * R)[:, None] // R) == (jnp.arange(gb * C)[None, :] // C)
        return jnp.where(same, y, 0.0).astype(BF16)

    lag8 = spread(lag.transpose(0, 1, 3, 2))
    bin8_re = spread(bin_re.transpose(0, 1, 3, 2)).reshape(nb, Lc * LANE, gb * N)
    bin8_im = spread(bin_im.transpose(0, 1, 3, 2)).reshape(nb, Lc * LANE, gb * N)
    cout8_re = spread(co_re.transpose(0, 1, 3, 2))
    cout8_im = spread((-co_im).transpose(0, 1, 3, 2))
    return (lag8, bin8_re, bin8_im, cout8_re, cout8_im,
            pw_re[Lc].reshape(nb, 1, gb * N), pw_im[Lc].reshape(nb, 1, gb * N))


def _s5_body(u_ref, lag_ref, bre_ref, bim_ref, cre_ref, cim_ref, are_ref, aim_ref, d_ref, z_ref,
             toep_s, cout_re_s, cout_im_s, xin_re, xin_im, xst_re, xst_im, car_re, car_im,
             *, n_chunks, batch):
    Lc = SSM_CHUNK

    @pl.when(pl.program_id(1) == 0)
    def _():
        car_re[...] = jnp.zeros_like(car_re)
        car_im[...] = jnp.zeros_like(car_im)
        toep_s[...] = jnp.zeros_like(toep_s)
        for ti in range(Lc):
            for to in range(ti, Lc):
                toep_s[ti * LANE:(ti + 1) * LANE, to * LANE:(to + 1) * LANE] = lag_ref[0, to - ti]
        for to in range(Lc):
            cout_re_s[:, to * LANE:(to + 1) * LANE] = cre_ref[0, to]
            cout_im_s[:, to * LANE:(to + 1) * LANE] = cim_ref[0, to]

    u = u_ref[0]
    xin_re[...] = jnp.dot(u, bre_ref[0], preferred_element_type=F32)
    xin_im[...] = jnp.dot(u, bim_ref[0], preferred_element_type=F32)
    a_re, a_im = are_ref[0], aim_ref[0]

    def step(c, carry):
        x_re, x_im = carry
        rows = pl.ds(pl.multiple_of(c * batch, batch), batch)
        xst_re[rows, :] = x_re
        xst_im[rows, :] = x_im
        return (a_re * x_re - a_im * x_im + xin_re[rows, :],
                a_re * x_im + a_im * x_re + xin_im[rows, :])

    x_re, x_im = lax.fori_loop(0, n_chunks, step, (car_re[...], car_im[...]))
    car_re[...] = x_re
    car_im[...] = x_im
    y = (jnp.dot(u, toep_s[...], preferred_element_type=F32)
         + jnp.dot(xst_re[...].astype(BF16), cout_re_s[...], preferred_element_type=F32)
         + jnp.dot(xst_im[...].astype(BF16), cout_im_s[...], preferred_element_type=F32))
    y = y + d_ref[0] * u.astype(F32)
    z_ref[0] = _gelu(y).astype(z_ref.dtype)


def _s5(u, ops, d_skip, B, S):
    lag8, bin_re, bin_im, cout_re, cout_im, a_re, a_im = ops
    nb, W, N = bin_re.shape
    Lc = SSM_CHUNK
    nc = S // Lc
    parts = 4 if nc % 4 == 0 else 1
    rows = (nc // parts) * B
    ut = u.reshape(B, nc, Lc, nb, LANE).transpose(3, 1, 0, 2, 4).reshape(nb, nc * B, W)
    d_t = jnp.tile(d_skip.reshape(nb, 1, LANE), (1, Lc, 1)).reshape(nb, 1, W)
    const = lambda shape: pl.BlockSpec((1,) + shape, lambda l, r: (l,) + (0,) * len(shape))
    block = pl.BlockSpec((1, rows, W), lambda l, r: (l, r, 0))
    zt = pl.pallas_call(
        functools.partial(_s5_body, n_chunks=nc // parts, batch=B),
        grid=(nb, parts),
        in_specs=[block, const((Lc, LANE, LANE)), const((W, N)), const((W, N)),
                  const((Lc, N, LANE)), const((Lc, N, LANE)), const((1, N)), const((1, N)),
                  const((1, W))],
        out_specs=block,
        out_shape=jax.ShapeDtypeStruct((nb, nc * B, W), BF16),
        scratch_shapes=([pltpu.VMEM((W, W), BF16), pltpu.VMEM((N, W), BF16), pltpu.VMEM((N, W), BF16)]
                        + [pltpu.VMEM((rows, N), F32)] * 4 + [pltpu.VMEM((B, N), F32)] * 2),
        compiler_params=_cparams(2, (W * W + 10 * W * N) * 2 + rows * W * 24 + rows * N * 32),
        name="s5_chunked_scan",
    )(ut, lag8, bin_re, bin_im, cout_re, cout_im, a_re, a_im, d_t)
    return zt.reshape(nb, nc, B, Lc, LANE).transpose(2, 1, 3, 0, 4).reshape(B * S, nb * LANE)


def _ffn_up_body(h_ref, wg_ref, wv_ref, cw_ref, cb_ref, o_ref, gate_ref, val_ref, tail_ref,
                 *, tm, seq, nj):
    i, j = pl.program_id(0), pl.program_id(1)

    def conv(g, shifted):
        y = cw_ref[FFN_CONV - 1:FFN_CONV, :] * g + cb_ref[...]
        for back in range(1, FFN_CONV):
            y = y + cw_ref[FFN_CONV - 1 - back:FFN_CONV - back, :] * shifted(back)
        return y

    def activate():
        jj = j - 1
        gate, val = gate_ref[...], val_ref[...]
        y = conv(gate, lambda back: pltpu.roll(gate, back, axis=0))
        o_ref[...] = (_gelu(y) * val).astype(o_ref.dtype)
        prev = jnp.where((i * tm) % seq == 0, 0.0, tail_ref[jj])
        top = gate[:SUBLANE, :]
        row = lax.broadcasted_iota(jnp.int32, top.shape, 0)
        y_top = conv(top, lambda back: jnp.where(row < back, pltpu.roll(prev, back, axis=0),
                                                 pltpu.roll(top, back, axis=0)))
        o_ref[:SUBLANE, :] = (_gelu(y_top) * val[:SUBLANE, :]).astype(o_ref.dtype)
        tail_ref[jj] = gate[tm - SUBLANE:, :]

    def multiply():
        n_parts = 8 if tm % 128 == 0 else 1
        part = tm // n_parts
        wg, wv = wg_ref[...].astype(BF16), wv_ref[...].astype(BF16)
        for rows in (slice(p * part, (p + 1) * part) for p in range(n_parts)):
            h = h_ref[rows, :]
            gate_ref[rows, :] = jnp.dot(h, wg, preferred_element_type=F32)
            val_ref[rows, :] = jnp.dot(h, wv, preferred_element_type=F32)

    @pl.when(jnp.logical_and(i == 0, j == 0))
    def _():
        tail_ref[...] = jnp.zeros_like(tail_ref)

    @pl.when(j == 0)
    def _():
        multiply()

    @pl.when(jnp.logical_and(j > 0, j < nj))
    def _():
        activate()
        multiply()

    @pl.when(j == nj)
    def _():
        activate()


def _ffn_up_act(h, w_up, layer, conv_w, conv_b, seq):
    T, D = h.shape
    Fd = w_up.shape[2] // 2
    tm = _pick_tile(seq, 1024, 16)
    tn = _pick_tile(Fd, 512, LANE)
    nj = Fd // tn
    cur = lambda j: jnp.minimum(j, nj - 1)
    lag = lambda j: jnp.maximum(j - 1, 0)
    w_bytes = jnp.dtype(w_up.dtype).itemsize
    need = 2 * tm * D * 2 + D * tn * (4 * w_bytes + 4) + 2 * tm * tn * 2 + 12 * tm * tn * 4
    return pl.pallas_call(
        functools.partial(_ffn_up_body, tm=tm, seq=seq, nj=nj),
        grid=(T // tm, nj + 1),
        in_specs=[pl.BlockSpec((tm, D), lambda i, j: (i, 0)),
                  pl.BlockSpec((None, D, tn), lambda i, j: (layer, 0, cur(j))),
                  pl.BlockSpec((None, D, tn), lambda i, j: (layer, 0, nj + cur(j))),
                  pl.BlockSpec((FFN_CONV, tn), lambda i, j: (0, lag(j))),
                  pl.BlockSpec((1, tn), lambda i, j: (0, lag(j)))],
        out_specs=pl.BlockSpec((tm, tn), lambda i, j: (i, lag(j))),
        out_shape=jax.ShapeDtypeStruct((T, Fd), BF16),
        scratch_shapes=[pltpu.VMEM((tm, tn), F32), pltpu.VMEM((tm, tn), F32),
                        pltpu.VMEM((nj, SUBLANE, tn), F32)],
        compiler_params=_cparams(2, need),
        name="ffn_up_convglu",
    )(h, w_up, w_up, conv_w, conv_b.reshape(1, Fd))


def _wkv_body(*refs, n_chunks, n_pairs, has_vmix):
    C = WKV_CHUNK
    nd = RWKV_HEAD_DIM
    lanes = 2 * nd
    n_seq = 8 if has_vmix else 6
    r_ref, w_ref, k_ref, v_ref, a_ref, g_ref = refs[:6]
    vf_ref, vg_ref = refs[6:8] if has_vmix else (None, None)
    kk_ref, ka_ref, rk_ref, lnw_ref, lnb_ref, o_ref = refs[n_seq:n_seq + 6]
    (st_ref, inv_s, t_s, arbk_s, left_s, dk_s, v_s, v2_s, bonus_s, dec_s, u_s, ls_s) = refs[n_seq + 6:]
    unroll = 2 if (n_chunks % 2 == 0 and n_pairs <= 8) else 1

    @pl.when(pl.program_id(2) == 0)
    def _():
        st_ref[...] = jnp.zeros_like(st_ref)

    lane_r = lax.broadcasted_iota(jnp.int32, (lanes, lanes), 0)
    lane_c = lax.broadcasted_iota(jnp.int32, (lanes, lanes), 1)
    same_head = (lane_r // nd) == (lane_c // nd)
    head_ones = same_head.astype(BF16)
    pos_r = lax.broadcasted_iota(jnp.int32, (C, lanes), 0)
    pos_c = lax.broadcasted_iota(jnp.int32, (C, lanes), 1) % C
    strict = pos_r > pos_c
    incl = pos_r >= pos_c
    eye = (pos_r == pos_c).astype(F32)
    tri = (lax.broadcasted_iota(jnp.int32, (C, C), 0)
           >= lax.broadcasted_iota(jnp.int32, (C, C), 1)).astype(BF16)
    lane1 = lax.broadcasted_iota(jnp.int32, (1, lanes), 1)
    m0 = (lane1 < nd).astype(F32)
    m1 = 1.0 - m0

    def split(x):
        hi = x.astype(BF16)
        return hi, (x - hi.astype(F32)).astype(BF16)

    def head_sum(x):
        n = x.shape[0]
        z = jnp.dot(jnp.concatenate(split(x), axis=0), head_ones, preferred_element_type=F32)
        return z[:n] + z[n:]

    def stack(x):
        return jnp.concatenate([x * m0, x * m1], axis=0).astype(BF16)

    def bdot(a, b, dims=None):
        a, b = a.astype(BF16), b.astype(BF16)
        if dims is None:
            return jnp.dot(a, b, preferred_element_type=F32)
        return lax.dot_general(a, b, dims, preferred_element_type=F32)

    def problems(i):
        out = []
        for cc in range(unroll):
            c = i * unroll + cc
            for p in range(n_pairs):
                out.append((c * n_pairs + p, pl.ds(pl.multiple_of(c * C, C), C),
                            slice(p * lanes, (p + 1) * lanes), p))
        return out

    def pass_a(i, _):
        prob = problems(i)
        ld = lambda ref: [ref[rows, cols] for _, rows, cols, _ in prob]
        par = lambda ref: [ref[:, cols] for _, _, cols, _ in prob]
        n = range(len(prob))
        r, k, v, a, wpre = ld(r_ref), ld(k_ref), ld(v_ref), ld(a_ref), ld(w_ref)
        k_k, k_a, r_k = par(kk_ref), par(ka_ref), par(rk_ref)
        if has_vmix:
            vf, vg = ld(vf_ref), ld(vg_ref)
            v = [v[j] + (vf[j] - v[j]) * vg[j] for j in n]
        dlog = [np.float32(-math.exp(-0.5)) * _sigmoid(w) for w in wpre]
        kk = [k[j] * k_k[j] for j in n]
        k2 = [k[j] * (1.0 + (a[j] - 1.0) * k_a[j]) for j in n]
        sums = [head_sum(jnp.concatenate([kk[j] * kk[j], r[j] * k2[j] * r_k[j]], axis=0)) for j in n]
        z = [jnp.dot(tri, jnp.concatenate(split(dlog[j]), axis=1), preferred_element_type=F32)
             for j in n]
        lg = [zz[:, :lanes] + zz[:, lanes:] for zz in z]
        kk = [kk[j] / jnp.maximum(jnp.sqrt(sums[j][:C]), 1e-12) for j in n]
        bv = [kk[j] * a[j] for j in n]
        e_neg = [jnp.exp(-l) for l in lg]
        e_end = [jnp.exp(l[C - 1:C, :] - l) for l in lg]
        left = [jnp.concatenate([-kk[j] * jnp.exp(lg[j] - dlog[j]), r[j] * jnp.exp(lg[j])],
                                axis=0).astype(BF16) for j in n]
        right = [jnp.concatenate([stack(bv[j] * e_neg[j]), stack(k2[j] * e_neg[j])], axis=0)
                 for j in n]
        aa = [bdot(left[j], right[j], NT_DIMS) for j in n]
        a_ab = [jnp.where(strict, m[:C, :lanes], 0.0) for m in aa]
        v2 = [stack(x) for x in v]
        t = [bdot(jnp.where(strict, aa[j][:C, lanes:], 0.0), v2[j]) for j in n]
        power = [bdot(m, stack(m)) for m in a_ab]
        inv = [eye + m for m in a_ab]
        for _ in range(2, int(math.log2(C))):
            z = [bdot(jnp.concatenate([inv[j], power[j]], axis=0), stack(power[j])) for j in n]
            inv = [inv[j] + z[j][:C] for j in n]
            power = [zz[C:] for zz in z]
        inv = [inv[j] + bdot(inv[j], stack(power[j])) for j in n]
        for j, (slot, rows, cols, _) in enumerate(prob):
            inv_s[slot] = inv[j].astype(BF16)
            t_s[slot] = t[j]
            arbk_s[slot] = jnp.where(jnp.concatenate([incl, incl], axis=1), aa[j][C:], 0.0).astype(BF16)
            left_s[slot] = left[j]
            dk_s[slot] = jnp.concatenate([bv[j] * e_end[j], k2[j] * e_end[j]], axis=0).astype(BF16)
            v_s[slot] = v[j].astype(BF16)
            v2_s[slot] = v2[j]
            bonus_s[slot] = sums[j][C:] * v[j]
            dec_s[slot] = jnp.exp(lg[j][C - 1:C, :])
        return 0

    def pass_b(c, _):
        n = range(n_pairs)
        slot = [c * n_pairs + p for p in n]
        state = [st_ref[p] for p in n]
        ls = [bdot(left_s[slot[p]], state[p], NT_DIMS) for p in n]
        u = [bdot(inv_s[slot[p]], stack(ls[p][:C] + t_s[slot[p]])) for p in n]
        u = [x.astype(BF16) for x in u]
        upd = [lax.dot_general(jnp.concatenate([u[p], v_s[slot[p]]], axis=0), dk_s[slot[p]], TN_DIMS,
                               preferred_element_type=F32) for p in n]
        for p in n:
            st_ref[p] = state[p] * dec_s[slot[p]] + jnp.where(same_head, upd[p], 0.0)
            u_s[slot[p]] = u[p]
            ls_s[slot[p]] = ls[p][C:]
        return 0

    def pass_c(i, _):
        prob = problems(i)
        n = range(len(prob))
        uv = [jnp.concatenate([stack(u_s[slot].astype(F32)), v2_s[slot]], axis=0)
              for slot, _, _, _ in prob]
        y = [ls_s[prob[j][0]] + jnp.dot(arbk_s[prob[j][0]], uv[j], preferred_element_type=F32)
             for j in n]
        mean = [head_sum(x) * np.float32(1.0 / nd) for x in y]
        d = [y[j] - mean[j] for j in n]
        var = [head_sum(x * x) * np.float32(1.0 / nd) for x in d]
        for j, (slot, rows, cols, _) in enumerate(prob):
            yn = d[j] * lax.rsqrt(var[j] + RWKV_GN_EPS) * lnw_ref[:, cols] + lnb_ref[:, cols]
            o_ref[rows, cols] = ((yn + bonus_s[slot]) * g_ref[rows, cols].astype(F32)).astype(o_ref.dtype)
        return 0

    lax.fori_loop(0, n_chunks // unroll, pass_a, 0)
    lax.fori_loop(0, n_chunks, pass_b, 0)
    lax.fori_loop(0, n_chunks // unroll, pass_c, 0)


def _wkv(r, wpre, k, v, a, g, v_first, v_gate, k_k, k_a, r_k, ln_w, ln_b, B, S):
    T, D = r.shape
    lanes = 2 * RWKV_HEAD_DIM
    has_vmix = v_first is not None
    n_pairs = next(n for n in (16, 8, 4, 1) if D % (n * lanes) == 0)
    width = n_pairs * lanes
    ts = _pick_tile(S, 4096 // n_pairs, WKV_CHUNK)
    nt = S // ts
    seq = pl.BlockSpec((ts, width), lambda b, h, t: (b * nt + t, h))
    par = pl.BlockSpec((1, width), lambda b, h, t: (0, h))
    args = [r, wpre, k, v, a, g] + ([v_first, v_gate] if has_vmix else [])
    params = [p.reshape(1, D) for p in (k_k, k_a, r_k, ln_w, ln_b)]
    C = WKV_CHUNK
    n_chunks = ts // C
    slots = n_chunks * n_pairs
    scratch = [pltpu.VMEM((n_pairs, lanes, lanes), F32),
               pltpu.VMEM((slots, C, lanes), BF16),
               pltpu.VMEM((slots, C, lanes), F32),
               pltpu.VMEM((slots, C, 2 * lanes), BF16),
               pltpu.VMEM((slots, 2 * C, lanes), BF16),
               pltpu.VMEM((slots, 2 * C, lanes), BF16),
               pltpu.VMEM((slots, C, lanes), BF16),
               pltpu.VMEM((slots, 2 * C, lanes), BF16),
               pltpu.VMEM((slots, C, lanes), F32),
               pltpu.VMEM((slots, 1, lanes), F32),
               pltpu.VMEM((slots, C, lanes), BF16),
               pltpu.VMEM((slots, C, lanes), F32)]
    scratch_bytes = slots * C * lanes * 30
    return pl.pallas_call(
        functools.partial(_wkv_body, n_chunks=n_chunks, n_pairs=n_pairs, has_vmix=has_vmix),
        grid=(B, D // width, nt),
        in_specs=[seq] * len(args) + [par] * len(params),
        out_specs=seq,
        out_shape=jax.ShapeDtypeStruct((T, D), BF16),
        scratch_shapes=scratch,
        compiler_params=_cparams(3, 2 * ts * width * 4 * (len(args) + 1) + scratch_bytes
                                 + 16 * 1024 * 1024),
        name="wkv7_chunked",
    )(*args, *params)


def _pad_to(x, n, axis):
    pad = n - x.shape[axis]
    if pad == 0:
        return x
    widths = [(0, 0)] * x.ndim
    widths[axis] = (0, pad)
    return jnp.pad(x, widths)


def _lora(x, w_down, w_up, *, bias=None, act_mid=None, act_out=None, out_dtype=F32, name="lora"):
    inner = -(-w_down.shape[1] // LANE) * LANE
    wd = _pad_to(w_down, inner, 1).astype(BF16)
    wu = _pad_to(w_up, inner, 0).astype(BF16)
    mid = _matmul([(x, wd)], out_dtype=BF16, act=act_mid, name=name + "_down")
    return _matmul([(mid, wu)], out_dtype=out_dtype, bias=bias, act=act_out, name=name + "_up")


def _stacked(w):
    if isinstance(w, tuple):
        return w
    return w.astype(BF16)[None], 0


def _hybrid_layer(x, B, S, norm, w_in, q_norm, k_norm, lam_re, lam_im, log_dt, b_re, b_im, c_re, c_im,
                  d_skip, glu_w, glu_b, w_out):
    (w_in, l_in), (glu_w, l_glu), (w_out, l_out) = _stacked(w_in), _stacked(glu_w), _stacked(w_out)
    sw = d_skip.shape[0]
    aw = w_out.shape[1] - sw
    assert aw == sw
    n_heads = aw // ATTN_HEAD_DIM
    h = _rmsnorm(x, norm)
    qkvu = _matmul([(h, w_in, l_in, 0)], out_dtype=BF16, name="hybrid_in_proj")
    y_attn = _attention(qkvu, q_norm, k_norm, B, S, n_heads)
    ops = _s5_chunk_operators(lam_re, lam_im, log_dt, b_re, b_im, c_re, c_im)
    z = _s5(qkvu[:, 3 * aw:], ops, d_skip, B, S)
    y_ssm = _matmul([(z, glu_w, l_glu, 0)], out_dtype=BF16, bias=glu_b, act="sigmoid", mul=z,
                    name="ssm_glu")
    return _matmul([(y_attn, w_out, l_out, 0), (y_ssm, w_out, l_out, 1)], out_dtype=F32, residual=x,
                   name="hybrid_out_proj")


def _rwkv_layer(x, B, S, norm, mu, w_r, w_k, w_v, w0, w1, w2, a0, a1, a2, g1, g2, k_k, k_a, r_k,
                ln_w, ln_b, w_o, v_first, v_mix):
    l_r, l_w, l_k, l_v, l_a, l_g = _rwkv_lerp(x, norm, mu, S)
    r = _matmul([(l_r, *_stacked(w_r), 0)], out_dtype=F32, name="rwkv_r")
    k = _matmul([(l_k, *_stacked(w_k), 0)], out_dtype=F32, name="rwkv_k")
    v = _matmul([(l_v, *_stacked(w_v), 0)], out_dtype=F32, name="rwkv_v")
    wpre = _lora(l_w, w1, w2, bias=w0, act_mid="tanh", name="rwkv_decay")
    a = _lora(l_a, a1, a2, bias=a0, act_out="sigmoid", name="rwkv_iclr")
    g = _lora(l_g, g1, g2, act_mid="sigmoid", out_dtype=BF16, name="rwkv_gate")
    if v_mix is None:
        v_gate, vf = None, None
        v_first = v
    else:
        v0, v1, v2 = v_mix
        v_gate = _lora(l_v, v1, v2, bias=v0, act_out="sigmoid", name="rwkv_vmix")
        vf = v_first
    y = _wkv(r, wpre, k, v, a, g, vf, v_gate, k_k, k_a, r_k.reshape(-1), ln_w, ln_b, B, S)
    return _matmul([(y, *_stacked(w_o), 0)], out_dtype=F32, residual=x, name="rwkv_out_proj"), v_first


def _ffn_layer(x, B, S, norm, w_up, conv_w, conv_b, w_down):
    h = _rmsnorm(x, norm)
    act = _ffn_up_act(h, *_stacked(w_up), conv_w, conv_b, S)
    return _matmul([(act, *_stacked(w_down), 0)], out_dtype=F32, residual=x, name="ffn_down")


def kernel(x, mix_norm, hy_w_in, attn_q_norm, attn_k_norm, ssm_lambda_re, ssm_lambda_im, ssm_log_dt, ssm_b_re, ssm_b_im, ssm_c_re, ssm_c_im, ssm_d, ssm_glu_w, ssm_glu_b, hy_w_out, rwkv_mu, rwkv_w_r, rwkv_w_k, rwkv_w_v, rwkv_w0, rwkv_w1, rwkv_w2, rwkv_a0, rwkv_a1, rwkv_a2, rwkv_g1, rwkv_g2, rwkv_k_k, rwkv_k_a, rwkv_r_k, rwkv_ln_w, rwkv_ln_b, rwkv_w_o, rwkv_v0, rwkv_v1, rwkv_v2, ffn_norm, ffn_w_up, ffn_conv_w, ffn_conv_b, ffn_w_down):
    B, S, D = x.shape
    depth = mix_norm.shape[0]
    x = x.reshape(B * S, D)
    hy_w_in, ssm_glu_w, hy_w_out, rwkv_w_r, rwkv_w_k, rwkv_w_v, rwkv_w_o, ffn_w_down = (
        w.astype(BF16) for w in (hy_w_in, ssm_glu_w, hy_w_out, rwkv_w_r, rwkv_w_k, rwkv_w_v,
                                 rwkv_w_o, ffn_w_down))
    v_first = None
    for layer in range(depth):
        i = layer // 2
        if layer % 2 == 0:
            x = _hybrid_layer(x, B, S, mix_norm[layer], (hy_w_in, i), attn_q_norm[i], attn_k_norm[i],
                              ssm_lambda_re[i], ssm_lambda_im[i], ssm_log_dt[i], ssm_b_re[i],
                              ssm_b_im[i], ssm_c_re[i], ssm_c_im[i], ssm_d[i], (ssm_glu_w, i),
                              ssm_glu_b[i], (hy_w_out, i))
        else:
            v_mix = None if i == 0 else (rwkv_v0[i - 1], rwkv_v1[i - 1], rwkv_v2[i - 1])
            x, v_first = _rwkv_layer(x, B, S, mix_norm[layer], rwkv_mu[i], (rwkv_w_r, i), (rwkv_w_k, i),
                                     (rwkv_w_v, i), rwkv_w0[i], rwkv_w1[i], rwkv_w2[i], rwkv_a0[i],
                                     rwkv_a1[i], rwkv_a2[i], rwkv_g1[i], rwkv_g2[i], rwkv_k_k[i],
                                     rwkv_k_a[i], rwkv_r_k[i], rwkv_ln_w[i], rwkv_ln_b[i],
                                     (rwkv_w_o, i), v_first, v_mix)
        x = _ffn_layer(x, B, S, ffn_norm[layer], (ffn_w_up, layer), ffn_conv_w[layer],
                       ffn_conv_b[layer], (ffn_w_down, layer))
    return x.reshape(B, S, D)
```

```python
import functools
import math

import numpy as np
import jax
import jax.numpy as jnp
from jax import lax
from jax.experimental import pallas as pl
from jax.experimental.pallas import tpu as pltpu

F32 = jnp.float32
BF16 = jnp.bfloat16
HIGHEST = lax.Precision.HIGHEST

RMS_EPS = 1e-6
ATTN_HEAD_DIM = 128
DILATED_BRANCHES = ((128, 1), (512, 4), (2048, 16))
ROPE_THETA = 10000.0
SSM_GROUP = 16
SSM_STATE = 64
SSM_CHUNK = 16
RWKV_HEAD_DIM = 64
RWKV_GN_EPS = 64e-5
WKV_CHUNK = 64
FFN_CONV = 3

V7X_VMEM_BYTES = 64 * 1024 * 1024
VMEM_BUDGET = 52 * 1024 * 1024
LANE = 128
SUBLANE = 8

NT_DIMS = (((1,), (1,)), ((), ()))
TN_DIMS = (((0,), (0,)), ((), ()))


def _cparams(n_grid, vmem_bytes):
    limit = int(min(max(vmem_bytes, 16 * 1024 * 1024), V7X_VMEM_BYTES - 6 * 1024 * 1024))
    return pltpu.CompilerParams(dimension_semantics=("arbitrary",) * n_grid,
                                vmem_limit_bytes=limit)


def _pick_tile(n, pref, align):
    t = min(pref, n)
    t -= t % align
    while t >= align:
        if n % t == 0:
            return t
        t -= align
    return n


def _sigmoid(x):
    return 1.0 / (1.0 + jnp.exp(-x))


def _gelu(x):
    return 0.5 * x * (1.0 + lax.erf(x * np.float32(math.sqrt(0.5))))


def _rms(x, gain):
    return x * lax.rsqrt(jnp.mean(x * x, axis=-1, keepdims=True) + RMS_EPS) * gain


def _rmsnorm_body(x_ref, g_ref, o_ref):
    o_ref[...] = _rms(x_ref[...], g_ref[...]).astype(o_ref.dtype)


def _rmsnorm(x, gain, out_dtype=BF16):
    T, D = x.shape
    tm = _pick_tile(T, 256, SUBLANE)
    return pl.pallas_call(
        _rmsnorm_body,
        grid=(T // tm,),
        in_specs=[pl.BlockSpec((tm, D), lambda i: (i, 0)),
                  pl.BlockSpec((1, D), lambda i: (0, 0))],
        out_specs=pl.BlockSpec((tm, D), lambda i: (i, 0)),
        out_shape=jax.ShapeDtypeStruct((T, D), out_dtype),
        compiler_params=_cparams(1, 6 * tm * D * 4),
        name="rmsnorm",
    )(x, gain.reshape(1, D))


def _lerp_body(x_ref, xh_ref, g_ref, mu_ref, *o_refs, tm, seq):
    i = pl.program_id(0)
    D = x_ref.shape[1]
    x, xh = x_ref[...], xh_ref[...]
    scale = lax.rsqrt(jnp.mean(x * x, axis=-1, keepdims=True) + RMS_EPS)
    scale_h = lax.rsqrt(jnp.mean(xh * xh, axis=-1, keepdims=True) + RMS_EPS)
    is_start = (i * tm) % seq == 0
    cw = _pick_tile(D, 512, LANE)
    for cols in (slice(c, c + cw) for c in range(0, D, cw)):
        g = g_ref[:, cols]
        h = x_ref[:, cols] * scale * g
        hh = xh_ref[:, cols] * scale_h * g
        prev = jnp.where(is_start, 0.0, hh[SUBLANE - 1:SUBLANE, :])
        hs = pltpu.roll(h, 1, axis=0)
        row = lax.broadcasted_iota(jnp.int32, h.shape, 0)
        xx = jnp.where(row == 0, prev, hs) - h
        for n, o_ref in enumerate(o_refs):
            o_ref[:, cols] = (h + xx * mu_ref[n:n + 1, cols]).astype(o_ref.dtype)


def _rwkv_lerp(x, gain, mu, seq):
    T, D = x.shape
    n = mu.shape[0]
    tm = _pick_tile(seq, 128, SUBLANE)
    hb = tm // SUBLANE
    return pl.pallas_call(
        functools.partial(_lerp_body, tm=tm, seq=seq),
        grid=(T // tm,),
        in_specs=[pl.BlockSpec((tm, D), lambda i: (i, 0)),
                  pl.BlockSpec((SUBLANE, D), lambda i: (jnp.maximum(i * hb - 1, 0), 0)),
                  pl.BlockSpec((1, D), lambda i: (0, 0)),
                  pl.BlockSpec((n, D), lambda i: (0, 0))],
        out_specs=[pl.BlockSpec((tm, D), lambda i: (i, 0))] * n,
        out_shape=[jax.ShapeDtypeStruct((T, D), BF16)] * n,
        compiler_params=_cparams(1, tm * D * (2 * 4 + 2 * 2 * n + 4 * 4)),
        name="rwkv_lerp",
    )(x, x, gain.reshape(1, D), mu)


def _mm_body(*refs, n_pairs, has_bias, act, has_mul, has_res):
    acc = None
    idx = 0
    for _ in range(n_pairs):
        d = jnp.dot(refs[idx][...], refs[idx + 1][...], preferred_element_type=F32)
        acc = d if acc is None else acc + d
        idx += 2
    if has_bias:
        acc = acc + refs[idx][...]
        idx += 1
    if act == "tanh":
        acc = jnp.tanh(acc)
    elif act == "sigmoid":
        acc = _sigmoid(acc)
    if has_mul:
        acc = acc * refs[idx][...].astype(F32)
        idx += 1
    if has_res:
        acc = acc + refs[idx][...]
        idx += 1
    o_ref = refs[idx]
    o_ref[...] = acc.astype(o_ref.dtype)


def _mm_tiles(M, N, k_total, out_bytes, extra_bytes):
    for tm_pref, tn_pref in ((1024, 1024), (1024, 512), (512, 512), (512, 256), (256, 256), (128, 256),
                             (128, 128)):
        tm = _pick_tile(M, tm_pref, 16)
        tn = _pick_tile(N, tn_pref, LANE)
        need = (2 * tm * k_total * 2 + 2 * k_total * tn * 2
                + tm * tn * (2 * out_bytes + 2 * extra_bytes + 8))
        if need <= VMEM_BUDGET:
            return tm, tn, need
    return tm, tn, need


def _matmul(pairs, *, out_dtype, bias=None, act=None, mul=None, residual=None, name="matmul"):
    M = pairs[0][0].shape[0]
    N = pairs[0][1].shape[-1]
    k_total = sum(p[0].shape[1] for p in pairs)
    out_bytes = jnp.dtype(out_dtype).itemsize
    extra = (2 if mul is not None else 0) + (4 if residual is not None else 0)
    tm, tn, need = _mm_tiles(M, N, k_total, out_bytes, extra)
    args, in_specs = [], []
    for pair in pairs:
        a, w = pair[0], pair[1]
        layer, kblock = (pair[2], pair[3]) if len(pair) == 4 else (0, 0)
        if w.ndim == 2:
            w = w[None]
        K = a.shape[1]
        assert a.dtype == BF16 and w.dtype == BF16 and a.shape[0] == M
        assert w.shape[2] == N and w.shape[1] % K == 0
        args += [a, w]
        in_specs += [pl.BlockSpec((tm, K), lambda i, j: (i, 0)),
                     pl.BlockSpec((None, K, tn), lambda i, j, layer=layer, kblock=kblock: (layer, kblock, j))]
    if bias is not None:
        args.append(bias.reshape(1, N).astype(F32))
        in_specs.append(pl.BlockSpec((1, tn), lambda i, j: (0, j)))
    if mul is not None:
        args.append(mul)
        in_specs.append(pl.BlockSpec((tm, tn), lambda i, j: (i, j)))
    if residual is not None:
        args.append(residual)
        in_specs.append(pl.BlockSpec((tm, tn), lambda i, j: (i, j)))
    return pl.pallas_call(
        functools.partial(_mm_body, n_pairs=len(pairs), has_bias=bias is not None, act=act,
                          has_mul=mul is not None, has_res=residual is not None),
        grid=(M // tm, N // tn),
        in_specs=in_specs,
        out_specs=pl.BlockSpec((tm, tn), lambda i, j: (i, j)),
        out_shape=jax.ShapeDtypeStruct((M, N), out_dtype),
        compiler_params=_cparams(2, need + 8 * 1024 * 1024),
        name=name,
    )(*args)


def _attn_bias(S, tq):
    delta = (S - tq) + np.arange(tq)[:, None] - np.arange(S)[None, :]
    mult = np.zeros(delta.shape, np.float64)
    for window, dilation in DILATED_BRANCHES:
        mult += (delta >= 0) & (delta <= window) & (delta % dilation == 0)
    bias = np.where(mult > 0, np.log2(np.maximum(mult, 1.0)), -1e30)
    return jnp.asarray(bias, F32)


def _rope_tables(S):
    half = ATTN_HEAD_DIM // 2
    inv_freq = ROPE_THETA ** (-jnp.arange(half, dtype=F32) / half)
    ang = jnp.arange(S, dtype=jnp.int32).astype(F32)[:, None] * inv_freq[None, :]
    cos, sin = jnp.cos(ang), jnp.sin(ang)
    return jnp.concatenate([cos, cos], axis=-1), jnp.concatenate([-sin, sin], axis=-1)


def _attn_body(q_ref, k_ref, v_ref, qg_ref, kg_ref, cos_ref, sin_ref, bias_ref, o_ref,
               qs_ref, ks_ref, *, S, tq, n_h):
    dh = ATTN_HEAD_DIM
    half = dh // 2
    heads = [slice(hh * dh, (hh + 1) * dh) for hh in range(n_h)]

    def prep(x_ref, g_ref, scale, lanes):
        x = _rms(x_ref[:, lanes].astype(F32), g_ref[...])
        x = x * cos_ref[...] + pltpu.roll(x, half, axis=1) * sin_ref[...]
        return (x * scale).astype(BF16)

    for lanes in heads:
        qs_ref[:, lanes] = prep(q_ref, qg_ref, np.float32(dh ** -0.5 * math.log2(math.e)), lanes)
        ks_ref[:, lanes] = prep(k_ref, kg_ref, np.float32(1.0), lanes)

    n_blocks = S // tq

    def scores(i, lanes):
        return lax.dot_general(qs_ref[i * tq:(i + 1) * tq, lanes], ks_ref[:(i + 1) * tq, lanes],
                               NT_DIMS, preferred_element_type=F32)

    s_next = [scores(0, lanes) for lanes in heads]
    for i in range(n_blocks):
        s = [x + bias_ref[:, (n_blocks - 1 - i) * tq:] for x in s_next]
        if i + 1 < n_blocks:
            s_next = [scores(i + 1, lanes) for lanes in heads]
        p = [jnp.exp2(x - jnp.max(x, axis=-1, keepdims=True)) for x in s]
        l = [jnp.sum(x, axis=-1, keepdims=True) for x in p]
        acc = [jnp.dot(p[hh].astype(BF16), v_ref[:(i + 1) * tq, heads[hh]],
                       preferred_element_type=F32) for hh in range(n_h)]
        for hh in range(n_h):
            o_ref[i * tq:(i + 1) * tq, heads[hh]] = (acc[hh] / l[hh]).astype(o_ref.dtype)


def _attention(qkvu, q_gain, k_gain, B, S, n_heads):
    T = B * S
    dh = ATTN_HEAD_DIM
    tq = _pick_tile(S, 256, LANE)
    n_h = 2 if n_heads % 2 == 0 else 1
    cos, sin = _rope_tables(S)
    bias = _attn_bias(S, tq)
    head = lambda off: pl.BlockSpec((S, n_h * dh), lambda b, h: (b, off // n_h + h))
    const2 = lambda shape: pl.BlockSpec(shape, lambda b, h: (0, 0))
    return pl.pallas_call(
        functools.partial(_attn_body, S=S, tq=tq, n_h=n_h),
        grid=(B, n_heads // n_h),
        in_specs=[head(0), head(n_heads), head(2 * n_heads),
                  const2((1, dh)), const2((1, dh)), const2((S, dh)), const2((S, dh)),
                  const2(bias.shape)],
        out_specs=pl.BlockSpec((S, n_h * dh), lambda b, h: (b, h)),
        out_shape=jax.ShapeDtypeStruct((T, n_heads * dh), BF16),
        scratch_shapes=[pltpu.VMEM((S, n_h * dh), BF16), pltpu.VMEM((S, n_h * dh), BF16)],
        compiler_params=_cparams(2, 2 * bias.size * 4 + 24 * n_h * S * dh * 4),
        name="dilated_attention",
    )(qkvu, qkvu, qkvu, q_gain.reshape(1, dh), k_gain.reshape(1, dh), cos, sin, bias)


def _s5_chunk_operators(lam_re, lam_im, log_dt, b_re, b_im, c_re, c_im):
    Lc = SSM_CHUNK
    G, N = lam_re.shape
    P = b_re.shape[-1]
    dt = jnp.exp(log_dt)[:, None]
    mag = jnp.exp(lam_re * dt)
    ab_re, ab_im = mag * jnp.cos(lam_im * dt), mag * jnp.sin(lam_im * dt)
    den = lam_re * lam_re + lam_im * lam_im
    nr, ni = ab_re - 1.0, ab_im
    coef_re = (nr * lam_re + ni * lam_im) / den
    coef_im = (ni * lam_re - nr * lam_im) / den
    bb_re = coef_re[..., None] * b_re - coef_im[..., None] * b_im
    bb_im = coef_re[..., None] * b_im + coef_im[..., None] * b_re
    steps = jnp.arange(Lc + 1, dtype=F32)[:, None, None]
    pmag = jnp.exp(lam_re * dt * steps)
    pw_re, pw_im = pmag * jnp.cos(lam_im * dt * steps), pmag * jnp.sin(lam_im * dt * steps)
    ca_re = c_re[None] * pw_re[:Lc, :, None, :] - c_im[None] * pw_im[:Lc, :, None, :]
    ca_im = c_re[None] * pw_im[:Lc, :, None, :] + c_im[None] * pw_re[:Lc, :, None, :]
    lag = (jnp.sum(ca_re[..., None] * bb_re[None, :, None], axis=3)
           - jnp.sum(ca_im[..., None] * bb_im[None, :, None], axis=3))
    rev = Lc - 1 - jnp.arange(Lc)
    bin_re = pw_re[rev][..., None] * bb_re[None] - pw_im[rev][..., None] * bb_im[None]
    bin_im = pw_re[rev][..., None] * bb_im[None] + pw_im[rev][..., None] * bb_re[None]
    nxt_re, nxt_im = pw_re[1:], pw_im[1:]
    co_re = c_re[None] * nxt_re[:, :, None, :] - c_im[None] * nxt_im[:, :, None, :]
    co_im = c_re[None] * nxt_im[:, :, None, :] + c_im[None] * nxt_re[:, :, None, :]

    gb = LANE // P
    nb = G // gb

    def spread(x):
        R, C = x.shape[2:]
        x = x.reshape(Lc, nb, gb, R, C).transpose(1, 0, 2, 3, 4).reshape(nb, Lc, gb * R, C)
        copies = jnp.tile(jnp.eye(C, dtype=BF16), (1, gb))
        y = jnp.dot(x.astype(BF16), copies, preferred_element_type=F32)
        same = (jnp.arange(gb * R)[:, None] // R) == (jnp.arange(gb * C)[None, :] // C)
        return jnp.where(same, y, 0.0).astype(BF16)

    lag8 = spread(lag.transpose(0, 1, 3, 2))
    bin8_re = spread(bin_re.transpose(0, 1, 3, 2)).reshape(nb, Lc * LANE, gb * N)
    bin8_im = spread(bin_im.transpose(0, 1, 3, 2)).reshape(nb, Lc * LANE, gb * N)
    cout8_re = spread(co_re.transpose(0, 1, 3, 2))
    cout8_im = spread((-co_im).transpose(0, 1, 3, 2))
    return (lag8, bin8_re, bin8_im, cout8_re, cout8_im,
            pw_re[Lc].reshape(nb, 1, gb * N), pw_im[Lc].reshape(nb, 1, gb * N))


def _s5_body(u_ref, lag_ref, bre_ref, bim_ref, cre_ref, cim_ref, are_ref, aim_ref, d_ref, z_ref,
             toep_s, cout_re_s, cout_im_s, xin_re, xin_im, xst_re, xst_im, car_re, car_im,
             *, n_chunks, batch):
    Lc = SSM_CHUNK

    @pl.when(pl.program_id(1) == 0)
    def _():
        car_re[...] = jnp.zeros_like(car_re)
        car_im[...] = jnp.zeros_like(car_im)
        toep_s[...] = jnp.zeros_like(toep_s)
        for ti in range(Lc):
            for to in range(ti, Lc):
                toep_s[ti * LANE:(ti + 1) * LANE, to * LANE:(to + 1) * LANE] = lag_ref[0, to - ti]
        for to in range(Lc):
            cout_re_s[:, to * LANE:(to + 1) * LANE] = cre_ref[0, to]
            cout_im_s[:, to * LANE:(to + 1) * LANE] = cim_ref[0, to]

    u = u_ref[0]
    xin_re[...] = jnp.dot(u, bre_ref[0], preferred_element_type=F32)
    xin_im[...] = jnp.dot(u, bim_ref[0], preferred_element_type=F32)
    a_re, a_im = are_ref[0], aim_ref[0]

    def step(c, carry):
        x_re, x_im = carry
        rows = pl.ds(pl.multiple_of(c * batch, batch), batch)
        xst_re[rows, :] = x_re
        xst_im[rows, :] = x_im
        return (a_re * x_re - a_im * x_im + xin_re[rows, :],
                a_re * x_im + a_im * x_re + xin_im[rows, :])

    x_re, x_im = lax.fori_loop(0, n_chunks, step, (car_re[...], car_im[...]))
    car_re[...] = x_re
    car_im[...] = x_im
    y = (jnp.dot(u, toep_s[...], preferred_element_type=F32)
         + jnp.dot(xst_re[...].astype(BF16), cout_re_s[...], preferred_element_type=F32)
         + jnp.dot(xst_im[...].astype(BF16), cout_im_s[...], preferred_element_type=F32))
    y = y + d_ref[0] * u.astype(F32)
    z_ref[0] = _gelu(y).astype(z_ref.dtype)


def _s5(u, ops, d_skip, B, S):
    lag8, bin_re, bin_im, cout_re, cout_im, a_re, a_im = ops
    nb, W, N = bin_re.shape
    Lc = SSM_CHUNK
    nc = S // Lc
    parts = 4 if nc % 4 == 0 else 1
    rows = (nc // parts) * B
    ut = u.reshape(B, nc, Lc, nb, LANE).transpose(3, 1, 0, 2, 4).reshape(nb, nc * B, W)
    d_t = jnp.tile(d_skip.reshape(nb, 1, LANE), (1, Lc, 1)).reshape(nb, 1, W)
    const = lambda shape: pl.BlockSpec((1,) + shape, lambda l, r: (l,) + (0,) * len(shape))
    block = pl.BlockSpec((1, rows, W), lambda l, r: (l, r, 0))
    zt = pl.pallas_call(
        functools.partial(_s5_body, n_chunks=nc // parts, batch=B),
        grid=(nb, parts),
        in_specs=[block, const((Lc, LANE, LANE)), const((W, N)), const((W, N)),
                  const((Lc, N, LANE)), const((Lc, N, LANE)), const((1, N)), const((1, N)),
                  const((1, W))],
        out_specs=block,
        out_shape=jax.ShapeDtypeStruct((nb, nc * B, W), BF16),
        scratch_shapes=([pltpu.VMEM((W, W), BF16), pltpu.VMEM((N, W), BF16), pltpu.VMEM((N, W), BF16)]
                        + [pltpu.VMEM((rows, N), F32)] * 4 + [pltpu.VMEM((B, N), F32)] * 2),
        compiler_params=_cparams(2, (W * W + 10 * W * N) * 2 + rows * W * 24 + rows * N * 32),
        name="s5_chunked_scan",
    )(ut, lag8, bin_re, bin_im, cout_re, cout_im, a_re, a_im, d_t)
    return zt.reshape(nb, nc, B, Lc, LANE).transpose(2, 1, 3, 0, 4).reshape(B * S, nb * LANE)


def _ffn_up_body(h_ref, wg_ref, wv_ref, cw_ref, cb_ref, o_ref, gate_ref, val_ref, tail_ref,
                 *, tm, seq, nj):
    i, j = pl.program_id(0), pl.program_id(1)

    def conv(g, shifted):
        y = cw_ref[FFN_CONV - 1:FFN_CONV, :] * g + cb_ref[...]
        for back in range(1, FFN_CONV):
            y = y + cw_ref[FFN_CONV - 1 - back:FFN_CONV - back, :] * shifted(back)
        return y

    def activate():
        jj = j - 1
        gate, val = gate_ref[...], val_ref[...]
        y = conv(gate, lambda back: pltpu.roll(gate, back, axis=0))
        o_ref[...] = (_gelu(y) * val).astype(o_ref.dtype)
        prev = jnp.where((i * tm) % seq == 0, 0.0, tail_ref[jj])
        top = gate[:SUBLANE, :]
        row = lax.broadcasted_iota(jnp.int32, top.shape, 0)
        y_top = conv(top, lambda back: jnp.where(row < back, pltpu.roll(prev, back, axis=0),
                                                 pltpu.roll(top, back, axis=0)))
        o_ref[:SUBLANE, :] = (_gelu(y_top) * val[:SUBLANE, :]).astype(o_ref.dtype)
        tail_ref[jj] = gate[tm - SUBLANE:, :]

    def multiply():
        n_parts = 8 if tm % 128 == 0 else 1
        part = tm // n_parts
        wg, wv = wg_ref[...].astype(BF16), wv_ref[...].astype(BF16)
        for rows in (slice(p * part, (p + 1) * part) for p in range(n_parts)):
            h = h_ref[rows, :]
            gate_ref[rows, :] = jnp.dot(h, wg, preferred_element_type=F32)
            val_ref[rows, :] = jnp.dot(h, wv, preferred_element_type=F32)

    @pl.when(jnp.logical_and(i == 0, j == 0))
    def _():
        tail_ref[...] = jnp.zeros_like(tail_ref)

    @pl.when(j == 0)
    def _():
        multiply()

    @pl.when(jnp.logical_and(j > 0, j < nj))
    def _():
        activate()
        multiply()

    @pl.when(j == nj)
    def _():
        activate()


def _ffn_up_act(h, w_up, layer, conv_w, conv_b, seq):
    T, D = h.shape
    Fd = w_up.shape[2] // 2
    tm = _pick_tile(seq, 1024, 16)
    tn = _pick_tile(Fd, 512, LANE)
    nj = Fd // tn
    cur = lambda j: jnp.minimum(j, nj - 1)
    lag = lambda j: jnp.maximum(j - 1, 0)
    w_bytes = jnp.dtype(w_up.dtype).itemsize
    need = 2 * tm * D * 2 + D * tn * (4 * w_bytes + 4) + 2 * tm * tn * 2 + 12 * tm * tn * 4
    return pl.pallas_call(
        functools.partial(_ffn_up_body, tm=tm, seq=seq, nj=nj),
        grid=(T // tm, nj + 1),
        in_specs=[pl.BlockSpec((tm, D), lambda i, j: (i, 0)),
                  pl.BlockSpec((None, D, tn), lambda i, j: (layer, 0, cur(j))),
                  pl.BlockSpec((None, D, tn), lambda i, j: (layer, 0, nj + cur(j))),
                  pl.BlockSpec((FFN_CONV, tn), lambda i, j: (0, lag(j))),
                  pl.BlockSpec((1, tn), lambda i, j: (0, lag(j)))],
        out_specs=pl.BlockSpec((tm, tn), lambda i, j: (i, lag(j))),
        out_shape=jax.ShapeDtypeStruct((T, Fd), BF16),
        scratch_shapes=[pltpu.VMEM((tm, tn), F32), pltpu.VMEM((tm, tn), F32),
                        pltpu.VMEM((nj, SUBLANE, tn), F32)],
        compiler_params=_cparams(2, need),
        name="ffn_up_convglu",
    )(h, w_up, w_up, conv_w, conv_b.reshape(1, Fd))


def _wkv_body(*refs, n_chunks, n_pairs, has_vmix):
    C = WKV_CHUNK
    nd = RWKV_HEAD_DIM
    lanes = 2 * nd
    n_seq = 8 if has_vmix else 6
    r_ref, w_ref, k_ref, v_ref, a_ref, g_ref = refs[:6]
    vf_ref, vg_ref = refs[6:8] if has_vmix else (None, None)
    kk_ref, ka_ref, rk_ref, lnw_ref, lnb_ref, o_ref = refs[n_seq:n_seq + 6]
    (st_ref, inv_s, t_s, arbk_s, left_s, dk_s, v_s, v2_s, bonus_s, dec_s, u_s, ls_s) = refs[n_seq + 6:]
    unroll = 2 if (n_chunks % 2 == 0 and n_pairs <= 8) else 1

    @pl.when(pl.program_id(2) == 0)
    def _():
        st_ref[...] = jnp.zeros_like(st_ref)

    lane_r = lax.broadcasted_iota(jnp.int32, (lanes, lanes), 0)
    lane_c = lax.broadcasted_iota(jnp.int32, (lanes, lanes), 1)
    same_head = (lane_r // nd) == (lane_c // nd)
    head_ones = same_head.astype(BF16)
    pos_r = lax.broadcasted_iota(jnp.int32, (C, lanes), 0)
    pos_c = lax.broadcasted_iota(jnp.int32, (C, lanes), 1) % C
    strict = pos_r > pos_c
    incl = pos_r >= pos_c
    eye = (pos_r == pos_c).astype(F32)
    tri = (lax.broadcasted_iota(jnp.int32, (C, C), 0)
           >= lax.broadcasted_iota(jnp.int32, (C, C), 1)).astype(BF16)
    lane1 = lax.broadcasted_iota(jnp.int32, (1, lanes), 1)
    m0 = (lane1 < nd).astype(F32)
    m1 = 1.0 - m0

    def split(x):
        hi = x.astype(BF16)
        return hi, (x - hi.astype(F32)).astype(BF16)

    def head_sum(x):
        n = x.shape[0]
        z = jnp.dot(jnp.concatenate(split(x), axis=0), head_ones, preferred_element_type=F32)
        return z[:n] + z[n:]

    def stack(x):
        return jnp.concatenate([x * m0, x * m1], axis=0).astype(BF16)

    def bdot(a, b, dims=None):
        a, b = a.astype(BF16), b.astype(BF16)
        if dims is None:
            return jnp.dot(a, b, preferred_element_type=F32)
        return lax.dot_general(a, b, dims, preferred_element_type=F32)

    def problems(i):
        out = []
        for cc in range(unroll):
            c = i * unroll + cc
            for p in range(n_pairs):
                out.append((c * n_pairs + p, pl.ds(pl.multiple_of(c * C, C), C),
                            slice(p * lanes, (p + 1) * lanes), p))
        return out

    def pass_a(i, _):
        prob = problems(i)
        ld = lambda ref: [ref[rows, cols] for _, rows, cols, _ in prob]
        par = lambda ref: [ref[:, cols] for _, _, cols, _ in prob]
        n = range(len(prob))
        r, k, v, a, wpre = ld(r_ref), ld(k_ref), ld(v_ref), ld(a_ref), ld(w_ref)
        k_k, k_a, r_k = par(kk_ref), par(ka_ref), par(rk_ref)
        if has_vmix:
            vf, vg = ld(vf_ref), ld(vg_ref)
            v = [v[j] + (vf[j] - v[j]) * vg[j] for j in n]
        dlog = [np.float32(-math.exp(-0.5)) * _sigmoid(w) for w in wpre]
        kk = [k[j] * k_k[j] for j in n]
        k2 = [k[j] * (1.0 + (a[j] - 1.0) * k_a[j]) for j in n]
        sums = [head_sum(jnp.concatenate([kk[j] * kk[j], r[j] * k2[j] * r_k[j]], axis=0)) for j in n]
        z = [jnp.dot(tri, jnp.concatenate(split(dlog[j]), axis=1), preferred_element_type=F32)
             for j in n]
        lg = [zz[:, :lanes] + zz[:, lanes:] for zz in z]
        kk = [kk[j] / jnp.maximum(jnp.sqrt(sums[j][:C]), 1e-12) for j in n]
        bv = [kk[j] * a[j] for j in n]
        e_neg = [jnp.exp(-l) for l in lg]
        e_end = [jnp.exp(l[C - 1:C, :] - l) for l in lg]
        left = [jnp.concatenate([-kk[j] * jnp.exp(lg[j] - dlog[j]), r[j] * jnp.exp(lg[j])],
                                axis=0).astype(BF16) for j in n]
        right = [jnp.concatenate([stack(bv[j] * e_neg[j]), stack(k2[j] * e_neg[j])], axis=0)
                 for j in n]
        aa = [bdot(left[j], right[j], NT_DIMS) for j in n]
        a_ab = [jnp.where(strict, m[:C, :lanes], 0.0) for m in aa]
        v2 = [stack(x) for x in v]
        t = [bdot(jnp.where(strict, aa[j][:C, lanes:], 0.0), v2[j]) for j in n]
        power = [bdot(m, stack(m)) for m in a_ab]
        inv = [eye + m for m in a_ab]
        for _ in range(2, int(math.log2(C))):
            z = [bdot(jnp.concatenate([inv[j], power[j]], axis=0), stack(power[j])) for j in n]
            inv = [inv[j] + z[j][:C] for j in n]
            power = [zz[C:] for zz in z]
        inv = [inv[j] + bdot(inv[j], stack(power[j])) for j in n]
        for j, (slot, rows, cols, _) in enumerate(prob):
            inv_s[slot] = inv[j].astype(BF16)
            t_s[slot] = t[j]
            arbk_s[slot] = jnp.where(jnp.concatenate([incl, incl], axis=1), aa[j][C:], 0.0).astype(BF16)
            left_s[slot] = left[j]
            dk_s[slot] = jnp.concatenate([bv[j] * e_end[j], k2[j] * e_end[j]], axis=0).astype(BF16)
            v_s[slot] = v[j].astype(BF16)
            v2_s[slot] = v2[j]
            bonus_s[slot] = sums[j][C:] * v[j]
            dec_s[slot] = jnp.exp(lg[j][C - 1:C, :])
        return 0

    def pass_b(c, _):
        n = range(n_pairs)
        slot = [c * n_pairs + p for p in n]
        state = [st_ref[p] for p in n]
        ls = [bdot(left_s[slot[p]], state[p], NT_DIMS) for p in n]
        u = [bdot(inv_s[slot[p]], stack(ls[p][:C] + t_s[slot[p]])) for p in n]
        u = [x.astype(BF16) for x in u]
        upd = [lax.dot_general(jnp.concatenate([u[p], v_s[slot[p]]], axis=0), dk_s[slot[p]], TN_DIMS,
                               preferred_element_type=F32) for p in n]
        for p in n:
            st_ref[p] = state[p] * dec_s[slot[p]] + jnp.where(same_head, upd[p], 0.0)
            u_s[slot[p]] = u[p]
            ls_s[slot[p]] = ls[p][C:]
        return 0

    def pass_c(i, _):
        prob = problems(i)
        n = range(len(prob))
        uv = [jnp.concatenate([stack(u_s[slot].astype(F32)), v2_s[slot]], axis=0)
              for slot, _, _, _ in prob]
        y = [ls_s[prob[j][0]] + jnp.dot(arbk_s[prob[j][0]], uv[j], preferred_element_type=F32)
             for j in n]
        mean = [head_sum(x) * np.float32(1.0 / nd) for x in y]
        d = [y[j] - mean[j] for j in n]
        var = [head_sum(x * x) * np.float32(1.0 / nd) for x in d]
        for j, (slot, rows, cols, _) in enumerate(prob):
            yn = d[j] * lax.rsqrt(var[j] + RWKV_GN_EPS) * lnw_ref[:, cols] + lnb_ref[:, cols]
            o_ref[rows, cols] = ((yn + bonus_s[slot]) * g_ref[rows, cols].astype(F32)).astype(o_ref.dtype)
        return 0

    lax.fori_loop(0, n_chunks // unroll, pass_a, 0)
    lax.fori_loop(0, n_chunks, pass_b, 0)
    lax.fori_loop(0, n_chunks // unroll, pass_c, 0)


def _wkv(r, wpre, k, v, a, g, v_first, v_gate, k_k, k_a, r_k, ln_w, ln_b, B, S):
    T, D = r.shape
    lanes = 2 * RWKV_HEAD_DIM
    has_vmix = v_first is not None
    n_pairs = next(n for n in (16, 8, 4, 1) if D % (n * lanes) == 0)
    width = n_pairs * lanes
    ts = _pick_tile(S, 4096 // n_pairs, WKV_CHUNK)
    nt = S // ts
    seq = pl.BlockSpec((ts, width), lambda b, h, t: (b * nt + t, h))
    par = pl.BlockSpec((1, width), lambda b, h, t: (0, h))
    args = [r, wpre, k, v, a, g] + ([v_first, v_gate] if has_vmix else [])
    params = [p.reshape(1, D) for p in (k_k, k_a, r_k, ln_w, ln_b)]
    C = WKV_CHUNK
    n_chunks = ts // C
    slots = n_chunks * n_pairs
    scratch = [pltpu.VMEM((n_pairs, lanes, lanes), F32),
               pltpu.VMEM((slots, C, lanes), BF16),
               pltpu.VMEM((slots, C, lanes), F32),
               pltpu.VMEM((slots, C, 2 * lanes), BF16),
               pltpu.VMEM((slots, 2 * C, lanes), BF16),
               pltpu.VMEM((slots, 2 * C, lanes), BF16),
               pltpu.VMEM((slots, C, lanes), BF16),
               pltpu.VMEM((slots, 2 * C, lanes), BF16),
               pltpu.VMEM((slots, C, lanes), F32),
               pltpu.VMEM((slots, 1, lanes), F32),
               pltpu.VMEM((slots, C, lanes), BF16),
               pltpu.VMEM((slots, C, lanes), F32)]
    scratch_bytes = slots * C * lanes * 30
    return pl.pallas_call(
        functools.partial(_wkv_body, n_chunks=n_chunks, n_pairs=n_pairs, has_vmix=has_vmix),
        grid=(B, D // width, nt),
        in_specs=[seq] * len(args) + [par] * len(params),
        out_specs=seq,
        out_shape=jax.ShapeDtypeStruct((T, D), BF16),
        scratch_shapes=scratch,
        compiler_params=_cparams(3, 2 * ts * width * 4 * (len(args) + 1) + scratch_bytes
                                 + 16 * 1024 * 1024),
        name="wkv7_chunked",
    )(*args, *params)


def _pad_to(x, n, axis):
    pad = n - x.shape[axis]
    if pad == 0:
        return x
    widths = [(0, 0)] * x.ndim
    widths[axis] = (0, pad)
    return jnp.pad(x, widths)


def _lora(x, w_down, w_up, *, bias=None, act_mid=None, act_out=None, out_dtype=F32, name="lora"):
    inner = -(-w_down.shape[1] // LANE) * LANE
    wd = _pad_to(w_down, inner, 1).astype(BF16)
    wu = _pad_to(w_up, inner, 0).astype(BF16)
    mid = _matmul([(x, wd)], out_dtype=BF16, act=act_mid, name=name + "_down")
    return _matmul([(mid, wu)], out_dtype=out_dtype, bias=bias, act=act_out, name=name + "_up")


def _stacked(w):
    if isinstance(w, tuple):
        return w
    return w.astype(BF16)[None], 0


def _hybrid_layer(x, B, S, norm, w_in, q_norm, k_norm, lam_re, lam_im, log_dt, b_re, b_im, c_re, c_im,
                  d_skip, glu_w, glu_b, w_out):
    (w_in, l_in), (glu_w, l_glu), (w_out, l_out) = _stacked(w_in), _stacked(glu_w), _stacked(w_out)
    sw = d_skip.shape[0]
    aw = w_out.shape[1] - sw
    assert aw == sw
    n_heads = aw // ATTN_HEAD_DIM
    h = _rmsnorm(x, norm)
    qkvu = _matmul([(h, w_in, l_in, 0)], out_dtype=BF16, name="hybrid_in_proj")
    y_attn = _attention(qkvu, q_norm, k_norm, B, S, n_heads)
    ops = _s5_chunk_operators(lam_re, lam_im, log_dt, b_re, b_im, c_re, c_im)
    z = _s5(qkvu[:, 3 * aw:], ops, d_skip, B, S)
    y_ssm = _matmul([(z, glu_w, l_glu, 0)], out_dtype=BF16, bias=glu_b, act="sigmoid", mul=z,
                    name="ssm_glu")
    return _matmul([(y_attn, w_out, l_out, 0), (y_ssm, w_out, l_out, 1)], out_dtype=F32, residual=x,
                   name="hybrid_out_proj")


def _rwkv_layer(x, B, S, norm, mu, w_r, w_k, w_v, w0, w1, w2, a0, a1, a2, g1, g2, k_k, k_a, r_k,
                ln_w, ln_b, w_o, v_first, v_mix):
    l_r, l_w, l_k, l_v, l_a, l_g = _rwkv_lerp(x, norm, mu, S)
    r = _matmul([(l_r, *_stacked(w_r), 0)], out_dtype=F32, name="rwkv_r")
    k = _matmul([(l_k, *_stacked(w_k), 0)], out_dtype=F32, name="rwkv_k")
    v = _matmul([(l_v, *_stacked(w_v), 0)], out_dtype=F32, name="rwkv_v")
    wpre = _lora(l_w, w1, w2, bias=w0, act_mid="tanh", name="rwkv_decay")
    a = _lora(l_a, a1, a2, bias=a0, act_out="sigmoid", name="rwkv_iclr")
    g = _lora(l_g, g1, g2, act_mid="sigmoid", out_dtype=BF16, name="rwkv_gate")
    if v_mix is None:
        v_gate, vf = None, None
        v_first = v
    else:
        v0, v1, v2 = v_mix
        v_gate = _lora(l_v, v1, v2, bias=v0, act_out="sigmoid", name="rwkv_vmix")
        vf = v_first
    y = _wkv(r, wpre, k, v, a, g, vf, v_gate, k_k, k_a, r_k.reshape(-1), ln_w, ln_b, B, S)
    return _matmul([(y, *_stacked(w_o), 0)], out_dtype=F32, residual=x, name="rwkv_out_proj"), v_first


def _ffn_layer(x, B, S, norm, w_up, conv_w, conv_b, w_down):
    h = _rmsnorm(x, norm)
    act = _ffn_up_act(h, *_stacked(w_up), conv_w, conv_b, S)
    return _matmul([(act, *_stacked(w_down), 0)], out_dtype=F32, residual=x, name="ffn_down")


def kernel(x, mix_norm, hy_w_in, attn_q_norm, attn_k_norm, ssm_lambda_re, ssm_lambda_im, ssm_log_dt, ssm_b_re, ssm_b_im, ssm_c_re, ssm_c_im, ssm_d, ssm_glu_w, ssm_glu_b, hy_w_out, rwkv_mu, rwkv_w_r, rwkv_w_k, rwkv_w_v, rwkv_w0, rwkv_w1, rwkv_w2, rwkv_a0, rwkv_a1, rwkv_a2, rwkv_g1, rwkv_g2, rwkv_k_k, rwkv_k_a, rwkv_r_k, rwkv_ln_w, rwkv_ln_b, rwkv_w_o, rwkv_v0, rwkv_v1, rwkv_v2, ffn_norm, ffn_w_up, ffn_conv_w, ffn_conv_b, ffn_w_down):
    B, S, D = x.shape
    depth = mix_norm.shape[0]
    x = x.reshape(B * S, D)
    hy_w_in, ssm_glu_w, hy_w_out, rwkv_w_r, rwkv_w_k, rwkv_w_v, rwkv_w_o, ffn_w_down = (
        w.astype(BF16) for w in (hy_w_in, ssm_glu_w, hy_w_out, rwkv_w_r, rwkv_w_k, rwkv_w_v,
                                 rwkv_w_o, ffn_w_down))
    v_first = None
    for layer in range(depth):
        i = layer // 2
        if layer % 2 == 0:
            x = _hybrid_layer(x, B, S, mix_norm[layer], (hy_w_in, i), attn_q_norm[i], attn_k_norm[i],
                              ssm_lambda_re[i], ssm_lambda_im[i], ssm_log_dt[i], ssm_b_re[i],
                              ssm_b_im[i], ssm_c_re[i], ssm_c_im[i], ssm_d[i], (ssm_glu_w, i),
                              ssm_glu_b[i], (hy_w_out, i))
        else:
            v_mix = None if i == 0 else (rwkv_v0[i - 1], rwkv_v1[i - 1], rwkv_v2[i - 1])
            x, v_first = _rwkv_layer(x, B, S, mix_norm[layer], rwkv_mu[i], (rwkv_w_r, i), (rwkv_w_k, i),
                                     (rwkv_w_v, i), rwkv_w0[i], rwkv_w1[i], rwkv_w2[i], rwkv_a0[i],
                                     rwkv_a1[i], rwkv_a2[i], rwkv_g1[i], rwkv_g2[i], rwkv_k_k[i],
                                     rwkv_k_a[i], rwkv_r_k[i], rwkv_ln_w[i], rwkv_ln_b[i],
                                     (rwkv_w_o, i), v_first, v_mix)
        x = _ffn_layer(x, B, S, ffn_norm[layer], (ffn_w_up, layer), ffn_conv_w[layer],
                       ffn_conv_b[layer], (ffn_w_down, layer))
    return x.reshape(B, S, D)
```
